```python
import jax, jax.numpy as jnp
from jax import lax
import numpy as np

D_MODEL = 2048
BATCH = 16
SEQ = 2048
DEPTH = 1

N_HEADS_DIFF = 8
DIFF_HEAD_DIM = 64
DIFF_V_DIM = 2 * DIFF_HEAD_DIM
DIFF_QK_WIDTH = N_HEADS_DIFF * 2 * DIFF_HEAD_DIM
DIFF_WIDTH = N_HEADS_DIFF * DIFF_V_DIM
N_HEADS_SB = 8
SB_HEAD_DIM = 128
SB_WIDTH = N_HEADS_SB * SB_HEAD_DIM
IN_PROJ_WIDTH = 2 * DIFF_QK_WIDTH + DIFF_WIDTH + 3 * SB_WIDTH
SPLIT_POINTS = (DIFF_QK_WIDTH, 2 * DIFF_QK_WIDTH, 2 * DIFF_QK_WIDTH + DIFF_WIDTH,
                2 * DIFF_QK_WIDTH + DIFF_WIDTH + SB_WIDTH,
                2 * DIFF_QK_WIDTH + DIFF_WIDTH + 2 * SB_WIDTH)
N_BRANCHES = 2
ROT_DIM = DIFF_HEAD_DIM // 4
ROPE_THETA = 500000.0
Q_BLOCK = 128
N_GROUPS = 4
EXPERTS_PER_GROUP = 8
N_EXPERTS = N_GROUPS * EXPERTS_PER_GROUP
TOP_K_IN_GROUP = 2
EXPERT_FF = D_MODEL // 4
EPS = 1e-6
NEG_INF = float(np.finfo(np.float32).min)

kernel_name = "hybrid_diffattn_stickbreak_hmoe"


def rms_norm(x, g):
    xf = x.astype(jnp.float32)
    y = xf * lax.rsqrt(jnp.mean(xf * xf, axis=-1, keepdims=True) + EPS)
    return (y * g.astype(jnp.float32)).astype(x.dtype)


def partial_rope(t, cos, sin):
    half = ROT_DIM // 2
    t1 = t[..., :half]
    t2 = t[..., half:ROT_DIM]
    return jnp.concatenate([t1 * cos - t2 * sin, t2 * cos + t1 * sin, t[..., ROT_DIM:]], axis=-1)


def diff_attention(q1, q2, k1, k2, v, lam):
    s_len = q1.shape[2]
    scale = DIFF_HEAD_DIM ** -0.5
    kpos = jnp.arange(s_len)

    def block(i):
        start = i * Q_BLOCK
        qpos = start + jnp.arange(Q_BLOCK)
        mask = kpos[None, :] <= qpos[:, None]

        def probs(q, k):
            qb = lax.dynamic_slice_in_dim(q, start, Q_BLOCK, axis=2)
            s = jnp.einsum('bhqd,bhkd->bhqk', qb, k) * scale
            return jax.nn.softmax(jnp.where(mask, s, NEG_INF), axis=-1)

        a = probs(q1, k1) - lam * probs(q2, k2)
        return jnp.einsum('bhqk,bhkd->bhqd', a, v)

    return lax.map(block, jnp.arange(s_len // Q_BLOCK))


def stick_breaking_attention(q, k, v):
    s_len = q.shape[2]
    scale = SB_HEAD_DIM ** -0.5
    kpos = jnp.arange(s_len)

    def block(i):
        start = i * Q_BLOCK
        qpos = start + jnp.arange(Q_BLOCK)
        mask = kpos[None, :] < qpos[:, None]
        qb = lax.dynamic_slice_in_dim(q, start, Q_BLOCK, axis=2)
        z = jnp.einsum('bhqd,bhkd->bhqk', qb, k) * scale
        log_beta = jax.nn.log_sigmoid(z)
        log_one_minus = jnp.where(mask, jax.nn.log_sigmoid(-z), 0.0)
        suffix = lax.cumsum(log_one_minus, axis=3, reverse=True) - log_one_minus
        a = jnp.where(mask, jnp.exp(log_beta + suffix), 0.0)
        return jnp.einsum('bhqk,bhkd->bhqd', a, v)

    return lax.map(block, jnp.arange(s_len // Q_BLOCK))


def merge_blocks(o):
    nb, b, h, blk, d = o.shape
    return o.transpose(1, 0, 3, 2, 4).reshape(b, nb * blk, h, d)


def setup_inputs(seed: int = 0) -> dict:
    key = jax.random.key(seed)
    ks = jax.random.split(key, 20)
    f32 = jnp.float32
    nrm = lambda k, shape, s: jax.random.normal(k, shape, f32) * s
    x = jax.random.normal(ks[0], (BATCH, SEQ, D_MODEL), f32)
    offsets = jax.random.randint(ks[1], (BATCH, 1), 0, 4096, dtype=jnp.int32)
    positions = offsets + jnp.arange(SEQ, dtype=jnp.int32)[None, :]
    return {
        "x": x,
        "positions": positions,
        "g_mix": 1.0 + nrm(ks[2], (DEPTH, D_MODEL), 0.01),
        "w_in": nrm(ks[3], (DEPTH, D_MODEL, IN_PROJ_WIDTH), D_MODEL ** -0.5),
        "w_gate": nrm(ks[4], (DEPTH, D_MODEL, N_BRANCHES * D_MODEL), D_MODEL ** -0.5),
        "lambda_params": nrm(ks[5], (DEPTH, 4, DIFF_HEAD_DIM), 0.1),
        "g_subln": 1.0 + nrm(ks[6], (DEPTH, DIFF_V_DIM), 0.01),
        "w_br_diff": nrm(ks[7], (DEPTH, DIFF_WIDTH, D_MODEL), DIFF_WIDTH ** -0.5),
        "w_br_sb": nrm(ks[8], (DEPTH, SB_WIDTH, D_MODEL), SB_WIDTH ** -0.5),
        "w_out": nrm(ks[9], (DEPTH, D_MODEL, D_MODEL), D_MODEL ** -0.5),
        "g_moe": 1.0 + nrm(ks[10], (DEPTH, D_MODEL), 0.01),
        "w_group": nrm(ks[11], (DEPTH, D_MODEL, N_GROUPS), D_MODEL ** -0.5),
        "b_group": nrm(ks[12], (DEPTH, N_GROUPS), 0.01),
        "w_router": nrm(ks[13], (DEPTH, D_MODEL, N_EXPERTS), D_MODEL ** -0.5),
        "b_router": nrm(ks[14], (DEPTH, N_EXPERTS), 0.01),
        "w_up_gate": nrm(ks[15], (DEPTH, N_EXPERTS, D_MODEL, EXPERT_FF), D_MODEL ** -0.5),
        "w_up": nrm(ks[16], (DEPTH, N_EXPERTS, D_MODEL, EXPERT_FF), D_MODEL ** -0.5),
        "w_down": nrm(ks[17], (DEPTH, N_EXPERTS, EXPERT_FF, D_MODEL), EXPERT_FF ** -0.5),
        "g_final": 1.0 + nrm(ks[18], (D_MODEL,), 0.01),
    }


def reference(x, positions, g_mix, w_in, w_gate, lambda_params, g_subln, w_br_diff, w_br_sb,
              w_out, g_moe, w_group, b_group, w_router, b_router, w_up_gate, w_up, w_down,
              g_final):
    f32 = jnp.float32
    b, s, d = x.shape
    t_tok = b * s
    inv_freq = ROPE_THETA ** (-jnp.arange(0, ROT_DIM, 2, dtype=f32) / ROT_DIM)
    ang = positions.astype(f32)[..., None] * inv_freq
    cos = jnp.cos(ang)[:, :, None, None, :]
    sin = jnp.sin(ang)[:, :, None, None, :]

    for l in range(DEPTH):
        h = rms_norm(x, g_mix[l])
        proj = h @ w_in[l]
        dq, dk, dv, sq, sk, sv = jnp.split(proj, SPLIT_POINTS, axis=-1)

        dq = partial_rope(dq.astype(f32).reshape(b, s, N_HEADS_DIFF, 2, DIFF_HEAD_DIM), cos, sin)
        dk = partial_rope(dk.astype(f32).reshape(b, s, N_HEADS_DIFF, 2, DIFF_HEAD_DIM), cos, sin)
        dq = dq.transpose(0, 2, 3, 1, 4)
        dk = dk.transpose(0, 2, 3, 1, 4)
        dv = dv.astype(f32).reshape(b, s, N_HEADS_DIFF, DIFF_V_DIM).transpose(0, 2, 1, 3)
        lp = lambda_params[l].astype(f32)
        lambda_init = 0.8 - 0.6 * np.exp(-0.3 * l)
        lam = jnp.exp(jnp.sum(lp[0] * lp[1])) - jnp.exp(jnp.sum(lp[2] * lp[3])) + lambda_init
        o_diff = merge_blocks(diff_attention(dq[:, :, 0], dq[:, :, 1], dk[:, :, 0], dk[:, :, 1], dv, lam))
        o_diff = rms_norm(o_diff, g_subln[l]) * (1.0 - lambda_init)
        y_diff = o_diff.reshape(b, s, DIFF_WIDTH).astype(x.dtype) @ w_br_diff[l]

        sq = sq.astype(f32).reshape(b, s, N_HEADS_SB, SB_HEAD_DIM).transpose(0, 2, 1, 3)
        sk = sk.astype(f32).reshape(b, s, N_HEADS_SB, SB_HEAD_DIM).transpose(0, 2, 1, 3)
        sv = sv.astype(f32).reshape(b, s, N_HEADS_SB, SB_HEAD_DIM).transpose(0, 2, 1, 3)
        o_sb = merge_blocks(stick_breaking_attention(sq, sk, sv))
        y_sb = o_sb.reshape(b, s, SB_WIDTH).astype(x.dtype) @ w_br_sb[l]

        gates = jax.nn.sigmoid((h @ w_gate[l]).astype(f32)).reshape(b, s, N_BRANCHES, d)
        mixed = (gates[:, :, 0] * y_diff.astype(f32) + gates[:, :, 1] * y_sb.astype(f32)).astype(x.dtype)
        x = x + mixed @ w_out[l]

        hn = rms_norm(x, g_moe[l]).reshape(t_tok, d)
        group_probs = jax.nn.softmax((hn @ w_group[l]).astype(f32) + b_group[l].astype(f32), axis=-1)
        g_val, g_idx = lax.top_k(group_probs, 1)
        e_logits = ((hn @ w_router[l]).astype(f32) + b_router[l].astype(f32)).reshape(
            t_tok, N_GROUPS, EXPERTS_PER_GROUP)
        e_sel = jnp.take_along_axis(e_logits, g_idx[:, :, None], axis=1)[:, 0]
        e_val, e_idx = lax.top_k(e_sel, TOP_K_IN_GROUP)
        e_w = jax.nn.softmax(e_val, axis=-1) * g_val
        flat_idx = g_idx * EXPERTS_PER_GROUP + e_idx
        combine = jnp.sum(jax.nn.one_hot(flat_idx, N_EXPERTS, dtype=f32) * e_w[..., None], axis=1)
        acc = jnp.zeros((t_tok, d), f32)
        for e in range(N_EXPERTS):
            y_e = (jax.nn.silu(hn @ w_up_gate[l, e]) * (hn @ w_up[l, e])) @ w_down[l, e]
            acc = acc + combine[:, e:e + 1] * y_e.astype(f32)
        x = x + acc.reshape(b, s, d).astype(x.dtype)

    return rms_norm(x, g_final)
```

```python
import functools

import numpy as np
import jax
import jax.numpy as jnp
from jax import lax
from jax.experimental import pallas as pl
from jax.experimental.pallas import tpu as pltpu

F32 = jnp.float32
BF16 = jnp.bfloat16

N_HEADS_DIFF = 8
DIFF_HEAD_DIM = 64
DIFF_V_DIM = 128
N_HEADS_SB = 8
SB_HEAD_DIM = 128
ROT_DIM = 16
ROPE_THETA = 500000.0
N_GROUPS = 4
EXPERTS_PER_GROUP = 8
N_EXPERTS = 32
EPS = 1e-6
NEG_INF = float(np.finfo(np.float32).min)
LAMBDA_INIT = 0.8 - 0.6 * float(np.exp(-0.3 * 0))

LANES = 128
VMEM_LIMIT = 56 * 1024 * 1024


def _cparams(sem):
    return pltpu.CompilerParams(dimension_semantics=sem, vmem_limit_bytes=VMEM_LIMIT)


def _rms(x, g):
    return x * lax.rsqrt(jnp.mean(x * x, axis=-1, keepdims=True) + EPS) * g


def _proj_kernel(x_ref, pos_ref, invf_ref, g_ref, w_ref, o_ref, h_ref, c_ref, s1_ref, s2_ref,
                 *, n_rope_tiles, n_plain_tiles):
    j = pl.program_id(1)
    tn = o_ref.shape[1]

    @pl.when(j == 0)
    def _():
        h_ref[...] = _rms(x_ref[...], g_ref[...]).astype(BF16)
        ang = pos_ref[...] * invf_ref[...]
        lane = lax.broadcasted_iota(jnp.int32, ang.shape, 1) % DIFF_HEAD_DIM
        cos = jnp.cos(ang)
        sin = jnp.sin(ang)
        half = ROT_DIM // 2
        c_ref[...] = jnp.where(lane < ROT_DIM, cos, 1.0)
        s1_ref[...] = jnp.where(lane < half, -sin, 0.0)
        s2_ref[...] = jnp.where((lane >= half) & (lane < ROT_DIM), sin, 0.0)

    acc = jnp.dot(h_ref[...], w_ref[...], preferred_element_type=F32)

    @pl.when(j < n_rope_tiles)
    def _():
        half = ROT_DIM // 2
        qscale = jnp.where(j == 0, DIFF_HEAD_DIM ** -0.5, 1.0).astype(F32)
        for c in range(tn // LANES):
            t = acc[:, c * LANES:(c + 1) * LANES]
            up = pltpu.roll(t, LANES - half, axis=1)
            dn = pltpu.roll(t, half, axis=1)
            r = t * c_ref[...] + up * s1_ref[...] + dn * s2_ref[...]
            o_ref[:, c * LANES:(c + 1) * LANES] = (r * qscale).astype(o_ref.dtype)

    @pl.when((j >= n_rope_tiles) & (j < n_rope_tiles + n_plain_tiles))
    def _():
        o_ref[...] = acc.astype(o_ref.dtype)

    @pl.when(j >= n_rope_tiles + n_plain_tiles)
    def _():
        o_ref[...] = jax.nn.sigmoid(acc).astype(o_ref.dtype)


def _proj(x2, posf, invf, g_mix, w_cat, *, tm, tn, n_rope_tiles, n_plain_tiles):
    t_tok, d = x2.shape
    n = w_cat.shape[1]
    kern = functools.partial(_proj_kernel, n_rope_tiles=n_rope_tiles, n_plain_tiles=n_plain_tiles)
    return pl.pallas_call(
        kern,
        grid=(t_tok // tm, n // tn),
        in_specs=[
            pl.BlockSpec((tm, d), lambda i, j: (i, 0)),
            pl.BlockSpec((tm, 1), lambda i, j: (i, 0)),
            pl.BlockSpec((1, LANES), lambda i, j: (0, 0)),
            pl.BlockSpec((1, d), lambda i, j: (0, 0)),
            pl.BlockSpec((d, tn), lambda i, j: (0, j)),
        ],
        out_specs=pl.BlockSpec((tm, tn), lambda i, j: (i, j)),
        out_shape=jax.ShapeDtypeStruct((t_tok, n), BF16),
        scratch_shapes=[
            pltpu.VMEM((tm, d), BF16),
            pltpu.VMEM((tm, LANES), F32),
            pltpu.VMEM((tm, LANES), F32),
            pltpu.VMEM((tm, LANES), F32),
        ],
        compiler_params=_cparams(("parallel", "arbitrary")),
        name="proj",
    )(x2, posf, invf, g_mix, w_cat)


def _dattn_kernel(q_ref, k_ref, v_ref, lp_ref, g_ref, o_ref, m_ref, l_ref, acc_ref, *, tq):
    s_len = q_ref.shape[1]
    nq = s_len // tq
    lp = lp_ref[...]
    lam = (jnp.exp(jnp.sum(lp[0:1] * lp[1:2], axis=-1, keepdims=True))
           - jnp.exp(jnp.sum(lp[2:3] * lp[3:4], axis=-1, keepdims=True)) + LAMBDA_INIT)
    lane = lax.broadcasted_iota(jnp.int32, (tq, LANES), 1)
    row = lax.broadcasted_iota(jnp.int32, (2 * tq, tq), 0) % tq
    col = lax.broadcasted_iota(jnp.int32, (2 * tq, tq), 1)
    causal = col <= row

    def q_body(qi, carry):
        q = q_ref[0, pl.ds(pl.multiple_of(qi * tq, tq), tq), :]
        zero = jnp.zeros_like(q)
        qq = jnp.concatenate([jnp.where(lane < DIFF_HEAD_DIM, q, zero),
                              jnp.where(lane >= DIFF_HEAD_DIM, q, zero)], axis=0)
        m_ref[...] = jnp.full(m_ref.shape, NEG_INF, F32)
        l_ref[...] = jnp.zeros(l_ref.shape, F32)
        acc_ref[...] = jnp.zeros(acc_ref.shape, F32)

        def tile(kj, masked):
            ks = pl.ds(pl.multiple_of(kj * tq, tq), tq)
            k = k_ref[0, ks, :]
            v = v_ref[0, ks, :]
            s = lax.dot_general(qq, k, (((1,), (1,)), ((), ())), preferred_element_type=F32)
            if masked:
                s = jnp.where(causal, s, NEG_INF)
            m_old = m_ref[...]
            m_new = jnp.maximum(m_old, jnp.max(s, axis=-1, keepdims=True))
            alpha = jnp.exp(m_old - m_new)
            p = jnp.exp(s - m_new)
            l_ref[...] = alpha * l_ref[...] + jnp.sum(p, axis=-1, keepdims=True)
            acc_ref[...] = alpha * acc_ref[...] + jnp.dot(p.astype(BF16), v, preferred_element_type=F32)
            m_ref[...] = m_new

        def k_body(kj, c):
            tile(kj, False)
            return c

        lax.fori_loop(0, qi, k_body, 0)
        tile(qi, True)

        o = acc_ref[...] / l_ref[...]
        od = o[:tq] - lam * o[tq:]
        od = _rms(od, g_ref[...]) * (1.0 - LAMBDA_INIT)
        o_ref[0, pl.ds(pl.multiple_of(qi * tq, tq), tq), :] = od.astype(o_ref.dtype)
        return carry

    lax.fori_loop(0, nq, q_body, 0)


def _dattn(proj3, lam_params, g_subln, *, tq):
    b, s_len, _ = proj3.shape
    h = N_HEADS_DIFF
    kern = functools.partial(_dattn_kernel, tq=tq)
    blk = lambda off: pl.BlockSpec((1, s_len, LANES), lambda bi, hi: (bi, 0, off + hi))
    return pl.pallas_call(
        kern,
        grid=(b, h),
        in_specs=[
            blk(0), blk(h), blk(2 * h),
            pl.BlockSpec(lam_params.shape, lambda bi, hi: (0, 0)),
            pl.BlockSpec((1, DIFF_V_DIM), lambda bi, hi: (0, 0)),
        ],
        out_specs=pl.BlockSpec((1, s_len, LANES), lambda bi, hi: (bi, 0, hi)),
        out_shape=jax.ShapeDtypeStruct((b, s_len, h * DIFF_V_DIM), BF16),
        scratch_shapes=[
            pltpu.VMEM((2 * tq, 1), F32),
            pltpu.VMEM((2 * tq, 1), F32),
            pltpu.VMEM((2 * tq, DIFF_V_DIM), F32),
        ],
        compiler_params=_cparams(("parallel", "parallel")),
        name="dattn",
    )(proj3, proj3, proj3, lam_params, g_subln)


def _sbattn_kernel(q_ref, k_ref, v_ref, u_ref, o_ref, r_ref, acc_ref, *, tq):
    s_len = q_ref.shape[1]
    nq = s_len // tq
    scale = SB_HEAD_DIM ** -0.5
    row = lax.broadcasted_iota(jnp.int32, (tq, tq), 0)
    col = lax.broadcasted_iota(jnp.int32, (tq, tq), 1)
    strict = col < row

    def q_body(qi, carry):
        q = q_ref[0, pl.ds(pl.multiple_of(qi * tq, tq), tq), :]
        r_ref[...] = jnp.zeros(r_ref.shape, F32)
        acc_ref[...] = jnp.zeros(acc_ref.shape, F32)

        def tile(kj, masked):
            ks = pl.ds(pl.multiple_of(kj * tq, tq), tq)
            k = k_ref[0, ks, :]
            v = v_ref[0, ks, :]
            z = lax.dot_general(q, k, (((1,), (1,)), ((), ())), preferred_element_type=F32) * scale
            l1p = jnp.log1p(jnp.exp(-jnp.abs(z)))
            log_beta = -(jnp.maximum(-z, 0.0) + l1p)
            lom = -(jnp.maximum(z, 0.0) + l1p)
            if masked:
                lom = jnp.where(strict, lom, 0.0)
            hi = lom.astype(BF16)
            lo = (lom - hi.astype(F32)).astype(BF16)
            suffix = (jnp.dot(hi, u_ref[...], preferred_element_type=F32)
                      + jnp.dot(lo, u_ref[...], preferred_element_type=F32))
            a = jnp.exp(log_beta + suffix + r_ref[...])
            if masked:
                a = jnp.where(strict, a, 0.0)
            acc_ref[...] += jnp.dot(a.astype(BF16), v, preferred_element_type=F32)
            r_ref[...] += jnp.sum(lom, axis=-1, keepdims=True)

        tile(qi, True)

        def k_body(n, c):
            tile(qi - 1 - n, False)
            return c

        lax.fori_loop(0, qi, k_body, 0)
        o_ref[0, pl.ds(pl.multiple_of(qi * tq, tq), tq), :] = acc_ref[...].astype(o_ref.dtype)
        return carry

    lax.fori_loop(0, nq, q_body, 0)


def _sbattn(proj3, tri, *, tq, col0):
    b, s_len, _ = proj3.shape
    h = N_HEADS_SB
    kern = functools.partial(_sbattn_kernel, tq=tq)
    blk = lambda off: pl.BlockSpec((1, s_len, LANES), lambda bi, hi: (bi, 0, off + hi))
    return pl.pallas_call(
        kern,
        grid=(b, h),
        in_specs=[
            blk(col0), blk(col0 + h), blk(col0 + 2 * h),
            pl.BlockSpec((tq, tq), lambda bi, hi: (0, 0)),
        ],
        out_specs=pl.BlockSpec((1, s_len, LANES), lambda bi, hi: (bi, 0, hi)),
        out_shape=jax.ShapeDtypeStruct((b, s_len, h * SB_HEAD_DIM), BF16),
        scratch_shapes=[
            pltpu.VMEM((tq, 1), F32),
            pltpu.VMEM((tq, SB_HEAD_DIM), F32),
        ],
        compiler_params=_cparams(("parallel", "parallel")),
        name="sbattn",
    )(proj3, proj3, proj3, tri)


def _split_bf16(a):
    hi = a.astype(BF16)
    return hi, (a - hi.astype(F32)).astype(BF16)


def _mix_kernel(od_ref, os_ref, g0_ref, g1_ref, x_ref, wd_ref, ws_ref, wo_ref, gm_ref,
                wr_hi_ref, wr_lo_ref, br_ref, x1_ref, hn_ref, comb_ref):
    yd = jnp.dot(od_ref[...], wd_ref[...], preferred_element_type=F32)
    ys = jnp.dot(os_ref[...], ws_ref[...], preferred_element_type=F32)
    mixed = g0_ref[...].astype(F32) * yd + g1_ref[...].astype(F32) * ys
    x1 = x_ref[...] + jnp.dot(mixed.astype(BF16), wo_ref[...], preferred_element_type=F32)
    x1_ref[...] = x1
    hn = _rms(x1, gm_ref[...])
    hn_hi, hn_lo = _split_bf16(hn)
    hn_ref[...] = hn_hi

    logits = (jnp.dot(hn_hi, wr_hi_ref[...], preferred_element_type=F32)
              + jnp.dot(hn_hi, wr_lo_ref[...], preferred_element_type=F32)
              + jnp.dot(hn_lo, wr_hi_ref[...], preferred_element_type=F32)) + br_ref[...]
    lane = lax.broadcasted_iota(jnp.int32, logits.shape, 1)
    big = jnp.int32(LANES)

    def first_argmax(vals, vmax):
        return jnp.min(jnp.where(vals == vmax, lane, big), axis=-1, keepdims=True)

    is_group = (lane >= N_EXPERTS) & (lane < N_EXPERTS + N_GROUPS)
    gl = jnp.where(is_group, logits, -jnp.inf)
    gmax = jnp.max(gl, axis=-1, keepdims=True)
    gexp = jnp.exp(gl - gmax)
    probs = gexp / jnp.sum(gexp, axis=-1, keepdims=True)
    g_val = jnp.max(probs, axis=-1, keepdims=True)
    g_idx = first_argmax(jnp.where(is_group, probs, -jnp.inf), g_val) - N_EXPERTS

    in_group = (lane >= g_idx * EXPERTS_PER_GROUP) & (lane < (g_idx + 1) * EXPERTS_PER_GROUP)
    el = jnp.where(in_group, logits, -jnp.inf)
    v1 = jnp.max(el, axis=-1, keepdims=True)
    i1 = first_argmax(el, v1)
    el2 = jnp.where(lane == i1, -jnp.inf, el)
    v2 = jnp.max(el2, axis=-1, keepdims=True)
    i2 = first_argmax(el2, v2)
    e2 = jnp.exp(v2 - v1)
    den = 1.0 + e2
    w1 = (1.0 / den) * g_val
    w2 = (e2 / den) * g_val
    comb_ref[...] = jnp.where(lane == i1, w1, 0.0) + jnp.where(lane == i2, w2, 0.0)


def _mix(o_diff, o_sb, proj, x2, w_brd, w_brs, w_out, g_moe, wr_hi, wr_lo, b_r, *, tm, gate_col0):
    t_tok, d = x2.shape
    wd_in = o_diff.shape[1]
    ws_in = o_sb.shape[1]
    gblk = gate_col0 // d
    const = lambda shape: pl.BlockSpec(shape, lambda i: (0, 0), pipeline_mode=pl.Buffered(1))
    return pl.pallas_call(
        _mix_kernel,
        grid=(t_tok // tm,),
        in_specs=[
            pl.BlockSpec((tm, wd_in), lambda i: (i, 0)),
            pl.BlockSpec((tm, ws_in), lambda i: (i, 0)),
            pl.BlockSpec((tm, d), lambda i: (i, gblk)),
            pl.BlockSpec((tm, d), lambda i: (i, gblk + 1)),
            pl.BlockSpec((tm, d), lambda i: (i, 0)),
            const(w_brd.shape), const(w_brs.shape), const(w_out.shape),
            const(g_moe.shape), const(wr_hi.shape), const(wr_lo.shape), const(b_r.shape),
        ],
        out_specs=[
            pl.BlockSpec((tm, d), lambda i: (i, 0)),
            pl.BlockSpec((tm, d), lambda i: (i, 0)),
            pl.BlockSpec((tm, LANES), lambda i: (i, 0)),
        ],
        out_shape=[
            jax.ShapeDtypeStruct((t_tok, d), F32),
            jax.ShapeDtypeStruct((t_tok, d), BF16),
            jax.ShapeDtypeStruct((t_tok, LANES), F32),
        ],
        compiler_params=_cparams(("parallel",)),
        name="mix",
    )(o_diff, o_sb, proj, proj, x2, w_brd, w_brs, w_out, g_moe, wr_hi, wr_lo, b_r)


def _moe_kernel(hn_ref, comb_ref, x1_ref, wg_ref, wu_ref, wd_ref, gf_ref, o_ref, acc_ref):
    e = pl.program_id(1)

    @pl.when(e == 0)
    def _():
        acc_ref[...] = jnp.zeros(acc_ref.shape, F32)

    hn = hn_ref[...]
    g = jnp.dot(hn, wg_ref[0], preferred_element_type=F32)
    u = jnp.dot(hn, wu_ref[0], preferred_element_type=F32)
    a = (jax.nn.silu(g) * u).astype(BF16)
    y = jnp.dot(a, wd_ref[0], preferred_element_type=F32)
    comb = comb_ref[...]
    lane = lax.broadcasted_iota(jnp.int32, comb.shape, 1)
    c = jnp.sum(jnp.where(lane == e, comb, 0.0), axis=-1, keepdims=True)
    acc_ref[...] += c * y

    @pl.when(e == pl.num_programs(1) - 1)
    def _():
        o_ref[...] = _rms(x1_ref[...] + acc_ref[...], gf_ref[...])


def _moe(hn, comb, x1, w_g, w_u, w_d, g_final, *, tm):
    t_tok, d = x1.shape
    n_e, _, ff = w_g.shape
    return pl.pallas_call(
        _moe_kernel,
        grid=(t_tok // tm, n_e),
        in_specs=[
            pl.BlockSpec((tm, d), lambda i, e: (i, 0)),
            pl.BlockSpec((tm, LANES), lambda i, e: (i, 0)),
            pl.BlockSpec((tm, d), lambda i, e: (i, 0)),
            pl.BlockSpec((1, d, ff), lambda i, e: (e, 0, 0)),
            pl.BlockSpec((1, d, ff), lambda i, e: (e, 0, 0)),
            pl.BlockSpec((1, ff, d), lambda i, e: (e, 0, 0)),
            pl.BlockSpec((1, d), lambda i, e: (0, 0)),
        ],
        out_specs=pl.BlockSpec((tm, d), lambda i, e: (i, 0)),
        out_shape=jax.ShapeDtypeStruct((t_tok, d), F32),
        scratch_shapes=[pltpu.VMEM((tm, d), F32)],
        compiler_params=_cparams(("parallel", "arbitrary")),
        name="moe",
    )(hn, comb, x1, w_g, w_u, w_d, g_final)


def _tile(n, pref):
    t = min(n, pref)
    assert n % t == 0, (n, t)
    return t


def kernel(x, positions, g_mix, w_in, w_gate, lambda_params, g_subln, w_br_diff, w_br_sb, w_out,
           g_moe, w_group, b_group, w_router, b_router, w_up_gate, w_up, w_down, g_final):
    b, s_len, d = x.shape
    t_tok = b * s_len
    depth = w_in.shape[0]
    assert depth == 1
    l = 0
    x2 = x.reshape(t_tok, d)

    inv_freq = ROPE_THETA ** (-jnp.arange(0, ROT_DIM, 2, dtype=F32) / ROT_DIM)
    lane = np.arange(LANES)
    invf = inv_freq[(lane % DIFF_HEAD_DIM) % (ROT_DIM // 2)].reshape(1, LANES)
    posf = positions.astype(F32).reshape(t_tok, 1)

    in_w = w_in.shape[2]
    w_cat = jnp.concatenate([w_in[l].astype(BF16), w_gate[l].astype(BF16)], axis=1)
    tn = 1024
    qk_w = N_HEADS_DIFF * 2 * DIFF_HEAD_DIM
    proj = _proj(x2, posf, invf, g_mix[l].reshape(1, d), w_cat,
                 tm=_tile(t_tok, 512), tn=tn,
                 n_rope_tiles=2 * qk_w // tn, n_plain_tiles=(in_w - 2 * qk_w) // tn)
    proj3 = proj.reshape(b, s_len, proj.shape[1])

    tq = _tile(s_len, 256)
    o_diff = _dattn(proj3, lambda_params[l], g_subln[l].reshape(1, DIFF_V_DIM), tq=tq)
    r = np.arange(tq)
    tri = jnp.asarray(r[:, None] > r[None, :], dtype=BF16)
    sb_col0 = (2 * qk_w + N_HEADS_DIFF * DIFF_V_DIM) // LANES
    o_sb = _sbattn(proj3, tri, tq=tq, col0=sb_col0)

    w_r = jnp.concatenate([w_router[l], w_group[l]], axis=1)
    w_r = jnp.pad(w_r, ((0, 0), (0, LANES - w_r.shape[1])))
    wr_hi = w_r.astype(BF16)
    wr_lo = (w_r - wr_hi.astype(F32)).astype(BF16)
    b_r = jnp.pad(jnp.concatenate([b_router[l], b_group[l]]), (0, LANES - N_EXPERTS - N_GROUPS))
    x1, hn, comb = _mix(
        o_diff.reshape(t_tok, -1), o_sb.reshape(t_tok, -1), proj, x2,
        w_br_diff[l].astype(BF16), w_br_sb[l].astype(BF16), w_out[l].astype(BF16),
        g_moe[l].reshape(1, d), wr_hi, wr_lo, b_r.reshape(1, LANES),
        tm=_tile(t_tok, 512), gate_col0=in_w)

    out = _moe(hn, comb, x1, w_up_gate[l].astype(BF16), w_up[l].astype(BF16), w_down[l].astype(BF16),
               g_final.reshape(1, d), tm=_tile(t_tok, 512))
    return out.reshape(b, s_len, d)
```

```python
import functools

import numpy as np
import jax
import jax.numpy as jnp
from jax import lax
from jax.experimental import pallas as pl
from jax.experimental.pallas import tpu as pltpu

F32 = jnp.float32
BF16 = jnp.bfloat16

N_HEADS_DIFF = 8
DIFF_HEAD_DIM = 64
DIFF_V_DIM = 128
N_HEADS_SB = 8
SB_HEAD_DIM = 128
ROT_DIM = 16
ROPE_THETA = 500000.0
N_GROUPS = 4
EXPERTS_PER_GROUP = 8
N_EXPERTS = 32
EPS = 1e-6
NEG_INF = float(np.finfo(np.float32).min)
SB_CUTOFF = -104.0
LAMBDA_INIT = 0.8 - 0.6 * float(np.exp(-0.3 * 0))
LOG2E = float(np.log2(np.e))

LANES = 128
VMEM_LIMIT = 56 * 1024 * 1024


def _cparams(sem):
    return pltpu.CompilerParams(dimension_semantics=sem, vmem_limit_bytes=VMEM_LIMIT)


def _rms(x, g):
    return x * lax.rsqrt(jnp.mean(x * x, axis=-1, keepdims=True) + EPS) * g


def _proj_kernel(x_ref, pos_ref, invf_ref, g_ref, w_ref, o_ref, h_ref, c_ref, s1_ref, s2_ref,
                 *, n_rope_tiles, n_plain_tiles, sb_q_tile):
    j = pl.program_id(1)
    tn = o_ref.shape[1]

    @pl.when(j == 0)
    def _():
        h_ref[...] = _rms(x_ref[...], g_ref[...]).astype(BF16)
        ang = pos_ref[...] * invf_ref[...]
        lane = lax.broadcasted_iota(jnp.int32, ang.shape, 1) % DIFF_HEAD_DIM
        cos = jnp.cos(ang)
        sin = jnp.sin(ang)
        half = ROT_DIM // 2
        c_ref[...] = jnp.where(lane < ROT_DIM, cos, 1.0)
        s1_ref[...] = jnp.where(lane < half, -sin, 0.0)
        s2_ref[...] = jnp.where((lane >= half) & (lane < ROT_DIM), sin, 0.0)

    acc = jnp.dot(h_ref[...], w_ref[...], preferred_element_type=F32)

    @pl.when(j < n_rope_tiles)
    def _():
        half = ROT_DIM // 2
        qscale = jnp.where(j == 0, DIFF_HEAD_DIM ** -0.5 * LOG2E, 1.0).astype(F32)
        for c in range(tn // LANES):
            t = acc[:, c * LANES:(c + 1) * LANES]
            up = pltpu.roll(t, LANES - half, axis=1)
            dn = pltpu.roll(t, half, axis=1)
            r = t * c_ref[...] + up * s1_ref[...] + dn * s2_ref[...]
            o_ref[:, c * LANES:(c + 1) * LANES] = (r * qscale).astype(o_ref.dtype)

    @pl.when((j >= n_rope_tiles) & (j < n_rope_tiles + n_plain_tiles))
    def _():
        scale = jnp.where(j == sb_q_tile, SB_HEAD_DIM ** -0.5, 1.0).astype(F32)
        o_ref[...] = (acc * scale).astype(o_ref.dtype)

    @pl.when(j >= n_rope_tiles + n_plain_tiles)
    def _():
        o_ref[...] = jax.nn.sigmoid(acc).astype(o_ref.dtype)


def _proj(x2, posf, invf, g_mix, w_cat, *, tm, tn, n_rope_tiles, n_plain_tiles, sb_q_tile):
    t_tok, d = x2.shape
    n = w_cat.shape[1]
    kern = functools.partial(_proj_kernel, n_rope_tiles=n_rope_tiles, n_plain_tiles=n_plain_tiles,
                             sb_q_tile=sb_q_tile)
    return pl.pallas_call(
        kern,
        grid=(t_tok // tm, n // tn),
        in_specs=[
            pl.BlockSpec((tm, d), lambda i, j: (i, 0)),
            pl.BlockSpec((tm, 1), lambda i, j: (i, 0)),
            pl.BlockSpec((1, LANES), lambda i, j: (0, 0)),
            pl.BlockSpec((1, d), lambda i, j: (0, 0)),
            pl.BlockSpec((d, tn), lambda i, j: (0, j)),
        ],
        out_specs=pl.BlockSpec((tm, tn), lambda i, j: (i, j)),
        out_shape=jax.ShapeDtypeStruct((t_tok, n), BF16),
        scratch_shapes=[
            pltpu.VMEM((tm, d), BF16),
            pltpu.VMEM((tm, LANES), F32),
            pltpu.VMEM((tm, LANES), F32),
            pltpu.VMEM((tm, LANES), F32),
        ],
        compiler_params=_cparams(("parallel", "arbitrary")),
        name="proj",
    )(x2, posf, invf, g_mix, w_cat)


def _dattn_kernel(q_ref, k_ref, v_ref, lp_ref, g_ref, o_ref, s_ref, ml_ref, mb_ref, acc_ref, vx_ref, *, tq):
    s_len = q_ref.shape[1]
    nq = s_len // tq
    nc = tq // LANES
    lp = lp_ref[...]
    lam = (jnp.exp(jnp.sum(lp[0:1] * lp[1:2], axis=-1, keepdims=True))
           - jnp.exp(jnp.sum(lp[2:3] * lp[3:4], axis=-1, keepdims=True)) + LAMBDA_INIT)
    lane = lax.broadcasted_iota(jnp.int32, (tq, LANES), 1)
    row = lax.broadcasted_iota(jnp.int32, (2 * tq, tq), 0) % tq
    col = lax.broadcasted_iota(jnp.int32, (2 * tq, tq), 1)
    causal = col <= row
    vx_ref[:, :DIFF_V_DIM] = v_ref[0]
    vx_ref[:, DIFF_V_DIM:] = jnp.ones((s_len, LANES), vx_ref.dtype)

    def q_body(qi, carry):
        q = q_ref[0, pl.ds(pl.multiple_of(qi * tq, tq), tq), :]
        zero = jnp.zeros_like(q)
        qq = jnp.concatenate([jnp.where(lane < DIFF_HEAD_DIM, q, zero),
                              jnp.where(lane >= DIFF_HEAD_DIM, q, zero)], axis=0)
        ml_ref[...] = jnp.full(ml_ref.shape, NEG_INF, F32)

        def scores(kj, masked):
            k = k_ref[0, pl.ds(pl.multiple_of(kj * tq, tq), tq), :]
            s = lax.dot_general(qq, k, (((1,), (1,)), ((), ())), preferred_element_type=F32)
            if masked:
                s = jnp.where(causal, s, NEG_INF)
            s_ref[kj] = s
            m = ml_ref[...]
            for c in range(nc):
                m = jnp.maximum(m, s[:, c * LANES:(c + 1) * LANES])
            ml_ref[...] = m

        def s_body(kj, c):
            scores(kj, False)
            return c

        lax.fori_loop(0, qi, s_body, 0)
        scores(qi, True)

        mb_ref[...] = jnp.broadcast_to(jnp.max(ml_ref[...], axis=-1, keepdims=True), mb_ref.shape)
        acc_ref[...] = jnp.zeros(acc_ref.shape, F32)

        def av_body(kj, c):
            vx = vx_ref[pl.ds(pl.multiple_of(kj * tq, tq), tq), :]
            mb = mb_ref[...]
            p = jnp.concatenate(
                [jnp.exp2(s_ref[kj, :, c * LANES:(c + 1) * LANES] - mb) for c in range(nc)], axis=1)
            acc_ref[...] += jnp.dot(p.astype(BF16), vx, preferred_element_type=F32)
            return c

        lax.fori_loop(0, qi + 1, av_body, 0)

        o = acc_ref[:, :DIFF_V_DIM] / acc_ref[:, DIFF_V_DIM:]
        od = o[:tq] - lam * o[tq:]
        od = _rms(od, g_ref[...]) * (1.0 - LAMBDA_INIT)
        o_ref[0, pl.ds(pl.multiple_of(qi * tq, tq), tq), :] = od.astype(o_ref.dtype)
        return carry

    lax.fori_loop(0, nq, q_body, 0)


def _dattn(proj3, lam_params, g_subln, *, tq):
    b, s_len, _ = proj3.shape
    h = N_HEADS_DIFF
    kern = functools.partial(_dattn_kernel, tq=tq)
    blk = lambda off: pl.BlockSpec((1, s_len, LANES), lambda bi, hi: (bi, 0, off + hi))
    return pl.pallas_call(
        kern,
        grid=(b, h),
        in_specs=[
            blk(0), blk(h), blk(2 * h),
            pl.BlockSpec(lam_params.shape, lambda bi, hi: (0, 0)),
            pl.BlockSpec((1, DIFF_V_DIM), lambda bi, hi: (0, 0)),
        ],
        out_specs=pl.BlockSpec((1, s_len, LANES), lambda bi, hi: (bi, 0, hi)),
        out_shape=jax.ShapeDtypeStruct((b, s_len, h * DIFF_V_DIM), BF16),
        scratch_shapes=[
            pltpu.VMEM((s_len // tq, 2 * tq, tq), F32),
            pltpu.VMEM((2 * tq, LANES), F32),
            pltpu.VMEM((2 * tq, LANES), F32),
            pltpu.VMEM((2 * tq, DIFF_V_DIM + LANES), F32),
            pltpu.VMEM((s_len, DIFF_V_DIM + LANES), BF16),
        ],
        compiler_params=_cparams(("parallel", "parallel")),
        name="dattn",
    )(proj3, proj3, proj3, lam_params, g_subln)


def _sb_terms(q, k, u, strict):
    z = lax.dot_general(q, k, (((1,), (1,)), ((), ())), preferred_element_type=F32)
    l1p = jnp.log(1.0 + jnp.exp(-jnp.abs(z)))
    lom = jnp.minimum(-z, 0.0) - l1p
    log_beta = lom + z
    if strict is not None:
        lom = jnp.where(strict, lom, 0.0)
    hi = lom.astype(BF16)
    lo = (lom - hi.astype(F32)).astype(BF16)
    suffix = jnp.dot(hi, u, preferred_element_type=F32) + jnp.dot(lo, u, preferred_element_type=F32)
    return log_beta, lom, suffix


def _lane_rowsum(x):
    return jnp.broadcast_to(jnp.sum(x, axis=-1, keepdims=True), (x.shape[0], LANES))


def _exp_shifted(x, rb):
    nc = x.shape[1] // LANES
    return jnp.concatenate([jnp.exp(x[:, c * LANES:(c + 1) * LANES] + rb) for c in range(nc)], axis=1)


def _sbattn_kernel(q_ref, k_ref, v_ref, u_ref, o_ref, rb_ref, acc_ref, *, tq):
    s_len = q_ref.shape[1]
    nq = s_len // tq
    row = lax.broadcasted_iota(jnp.int32, (tq, tq), 0)
    col = lax.broadcasted_iota(jnp.int32, (tq, tq), 1)
    strict = col < row

    def rows(i):
        return pl.ds(pl.multiple_of(i * tq, tq), tq)

    def q_tile(qi, first):
        q = q_ref[0, rows(qi), :]
        u = u_ref[...]
        log_beta, lom, suffix = _sb_terms(q, k_ref[0, rows(qi), :], u, strict)
        a = jnp.where(strict, jnp.exp(log_beta + suffix), 0.0)
        acc = jnp.dot(a.astype(BF16), v_ref[0, rows(qi), :], preferred_element_type=F32)
        rb = _lane_rowsum(lom)
        if first:
            o_ref[0, rows(qi), :] = acc.astype(o_ref.dtype)
            return
        log_beta, lom, suffix = _sb_terms(q, k_ref[0, rows(qi - 1), :], u, None)
        a = _exp_shifted(log_beta + suffix, rb)
        acc_ref[...] = acc + jnp.dot(a.astype(BF16), v_ref[0, rows(qi - 1), :], preferred_element_type=F32)
        rb = rb + _lane_rowsum(lom)
        rb_ref[...] = rb

        def cond(c):
            n, rmax = c
            return (n <= qi) & (rmax > SB_CUTOFF)

        def body(c):
            n, _ = c
            kj = qi - n
            log_beta, lom, suffix = _sb_terms(q_ref[0, rows(qi), :], k_ref[0, rows(kj), :], u_ref[...], None)
            a = _exp_shifted(log_beta + suffix, rb_ref[...])
            acc_ref[...] += jnp.dot(a.astype(BF16), v_ref[0, rows(kj), :], preferred_element_type=F32)
            rb_new = rb_ref[...] + _lane_rowsum(lom)
            rb_ref[...] = rb_new
            return n + 1, jnp.max(rb_new)

        lax.while_loop(cond, body, (jnp.int32(2), jnp.max(rb)))
        o_ref[0, rows(qi), :] = acc_ref[...].astype(o_ref.dtype)

    q_tile(0, True)

    def q_body(qi, carry):
        q_tile(qi, False)
        return carry

    lax.fori_loop(1, nq, q_body, 0)


def _sbattn(proj3, tri, *, tq, col0):
    b, s_len, _ = proj3.shape
    h = N_HEADS_SB
    kern = functools.partial(_sbattn_kernel, tq=tq)
    blk = lambda off: pl.BlockSpec((1, s_len, LANES), lambda bi, hi: (bi, 0, off + hi))
    return pl.pallas_call(
        kern,
        grid=(b, h),
        in_specs=[
            blk(col0), blk(col0 + h), blk(col0 + 2 * h),
            pl.BlockSpec((tq, tq), lambda bi, hi: (0, 0)),
        ],
        out_specs=pl.BlockSpec((1, s_len, LANES), lambda bi, hi: (bi, 0, hi)),
        out_shape=jax.ShapeDtypeStruct((b, s_len, h * SB_HEAD_DIM), BF16),
        scratch_shapes=[
            pltpu.VMEM((tq, LANES), F32),
            pltpu.VMEM((tq, SB_HEAD_DIM), F32),
        ],
        compiler_params=_cparams(("parallel", "parallel")),
        name="sbattn",
    )(proj3, proj3, proj3, tri)


def _split_bf16(a):
    hi = a.astype(BF16)
    return hi, (a - hi.astype(F32)).astype(BF16)


def _mix_kernel(od_ref, os_ref, g0_ref, g1_ref, x_ref, wd_ref, ws_ref, wo_ref, gm_ref,
                wr_hi_ref, wr_lo_ref, br_ref, ltri_ref, x1_ref, hn_ref, route_ref, cnt_ref):
    @pl.when(pl.program_id(0) == 0)
    def _():
        cnt_ref[...] = jnp.zeros(cnt_ref.shape, F32)

    yd = jnp.dot(od_ref[...], wd_ref[...], preferred_element_type=F32)
    ys = jnp.dot(os_ref[...], ws_ref[...], preferred_element_type=F32)
    mixed = g0_ref[...].astype(F32) * yd + g1_ref[...].astype(F32) * ys
    x1 = x_ref[...] + jnp.dot(mixed.astype(BF16), wo_ref[...], preferred_element_type=F32)
    x1_ref[...] = x1
    hn = _rms(x1, gm_ref[...])
    hn_hi, hn_lo = _split_bf16(hn)
    hn_ref[...] = hn_hi.astype(hn_ref.dtype)

    logits = (jnp.dot(hn_hi, wr_hi_ref[...], preferred_element_type=F32)
              + jnp.dot(hn_hi, wr_lo_ref[...], preferred_element_type=F32)
              + jnp.dot(hn_lo, wr_hi_ref[...], preferred_element_type=F32)) + br_ref[...]
    lane = lax.broadcasted_iota(jnp.int32, logits.shape, 1)
    big = jnp.int32(LANES)

    def first_argmax(vals, vmax):
        return jnp.min(jnp.where(vals == vmax, lane, big), axis=-1, keepdims=True)

    is_group = (lane >= N_EXPERTS) & (lane < N_EXPERTS + N_GROUPS)
    gl = jnp.where(is_group, logits, -jnp.inf)
    gmax = jnp.max(gl, axis=-1, keepdims=True)
    gexp = jnp.exp(gl - gmax)
    probs = gexp / jnp.sum(gexp, axis=-1, keepdims=True)
    g_val = jnp.max(probs, axis=-1, keepdims=True)
    g_idx = first_argmax(jnp.where(is_group, probs, -jnp.inf), g_val) - N_EXPERTS

    in_group = (lane >= g_idx * EXPERTS_PER_GROUP) & (lane < (g_idx + 1) * EXPERTS_PER_GROUP)
    el = jnp.where(in_group, logits, -jnp.inf)
    v1 = jnp.max(el, axis=-1, keepdims=True)
    i1 = first_argmax(el, v1)
    el2 = jnp.where(lane == i1, -jnp.inf, el)
    v2 = jnp.max(el2, axis=-1, keepdims=True)
    i2 = first_argmax(el2, v2)
    e2 = jnp.exp(v2 - v1)
    den = 1.0 + e2
    w1 = (1.0 / den) * g_val
    w2 = (e2 / den) * g_val

    pick1 = lane == i1
    pick2 = lane == i2
    onehot = jnp.where(pick1 | pick2, 1.0, 0.0).astype(BF16)
    seen = jnp.dot(ltri_ref[...], onehot, preferred_element_type=F32) + cnt_ref[...]
    rank1 = jnp.sum(jnp.where(pick1, seen, 0.0), axis=-1, keepdims=True) - 1.0
    rank2 = jnp.sum(jnp.where(pick2, seen, 0.0), axis=-1, keepdims=True) - 1.0
    cnt_ref[...] = seen[seen.shape[0] - 1:, :]
    cols = (w1, w2, i1.astype(F32), i2.astype(F32), rank1, rank2)
    route = jnp.zeros(logits.shape, F32)
    for c, val in enumerate(cols):
        route = jnp.where(lane == c, val, route)
    route_ref[...] = route


def _mix(o_diff, o_sb, proj, x2, w_brd, w_brs, w_out, g_moe, wr_hi, wr_lo, b_r, *, tm, gate_col0):
    t_tok, d = x2.shape
    wd_in = o_diff.shape[1]
    ws_in = o_sb.shape[1]
    gblk = gate_col0 // d
    r = np.arange(tm)
    ltri = jnp.asarray(r[None, :] <= r[:, None], dtype=BF16)
    const = lambda shape: pl.BlockSpec(shape, lambda i: (0, 0), pipeline_mode=pl.Buffered(1))
    return pl.pallas_call(
        _mix_kernel,
        grid=(t_tok // tm,),
        in_specs=[
            pl.BlockSpec((tm, wd_in), lambda i: (i, 0)),
            pl.BlockSpec((tm, ws_in), lambda i: (i, 0)),
            pl.BlockSpec((tm, d), lambda i: (i, gblk)),
            pl.BlockSpec((tm, d), lambda i: (i, gblk + 1)),
            pl.BlockSpec((tm, d), lambda i: (i, 0)),
            const(w_brd.shape), const(w_brs.shape), const(w_out.shape),
            const(g_moe.shape), const(wr_hi.shape), const(wr_lo.shape), const(b_r.shape),
            const(ltri.shape),
        ],
        out_specs=[
            pl.BlockSpec((tm, d), lambda i: (i, 0)),
            pl.BlockSpec((tm, d), lambda i: (i, 0)),
            pl.BlockSpec((tm, LANES), lambda i: (i, 0)),
            pl.BlockSpec((1, LANES), lambda i: (0, 0)),
        ],
        out_shape=[
            jax.ShapeDtypeStruct((t_tok, d), F32),
            jax.ShapeDtypeStruct((t_tok, d), F32),
            jax.ShapeDtypeStruct((t_tok, LANES), F32),
            jax.ShapeDtypeStruct((1, LANES), F32),
        ],
        compiler_params=_cparams(("arbitrary",)),
        name="mix",
    )(o_diff, o_sb, proj, proj, x2, w_brd, w_brs, w_out, g_moe, wr_hi, wr_lo, b_r, ltri)


def _gather_rows(src_hbm, idx_ref, base, dst, sem, n_rows):
    for r in range(n_rows):
        row = idx_ref[base + r]
        pltpu.make_async_copy(src_hbm.at[pl.ds(row, 1), :], dst.at[pl.ds(r, 1), :], sem).start()


def _wait_rows(src_hbm, dst, sem):
    pltpu.make_async_copy(src_hbm.at[pl.ds(0, dst.shape[0]), :], dst, sem).wait()


def _experts_kernel(te_ref, nused_ref, src_ref, hn_hbm, wg_ref, wu_ref, wd_ref, y_ref, xbuf, sem):
    i = pl.program_id(0)
    tm = xbuf.shape[1]
    n_used = nused_ref[0]
    slot = i % 2

    @pl.when(i == 0)
    def _():
        _gather_rows(hn_hbm, src_ref, 0, xbuf.at[0], sem.at[0], tm)

    @pl.when(i + 1 < n_used)
    def _():
        _gather_rows(hn_hbm, src_ref, (i + 1) * tm, xbuf.at[1 - slot], sem.at[1 - slot], tm)

    @pl.when(i < n_used)
    def _():
        _wait_rows(hn_hbm, xbuf.at[slot], sem.at[slot])
        x = xbuf[slot].astype(BF16)
        g = jnp.dot(x, wg_ref[0], preferred_element_type=F32)
        u = jnp.dot(x, wu_ref[0], preferred_element_type=F32)
        a = (jax.nn.silu(g) * u).astype(BF16)
        y_ref[...] = jnp.dot(a, wd_ref[0], preferred_element_type=F32)

    @pl.when(i >= n_used)
    def _():
        y_ref[...] = jnp.zeros(y_ref.shape, y_ref.dtype)


def _experts(tile_expert, n_used, src, hn, w_g, w_u, w_d, *, tm):
    n_rows = src.shape[0]
    d = hn.shape[1]
    n_e, _, ff = w_g.shape
    grid_spec = pltpu.PrefetchScalarGridSpec(
        num_scalar_prefetch=3,
        grid=(n_rows // tm,),
        in_specs=[
            pl.BlockSpec(memory_space=pl.ANY),
            pl.BlockSpec((1, d, ff), lambda i, te, nu, sr: (te[i], 0, 0)),
            pl.BlockSpec((1, d, ff), lambda i, te, nu, sr: (te[i], 0, 0)),
            pl.BlockSpec((1, ff, d), lambda i, te, nu, sr: (te[i], 0, 0)),
        ],
        out_specs=pl.BlockSpec((tm, d), lambda i, te, nu, sr: (i, 0)),
        scratch_shapes=[pltpu.VMEM((2, tm, d), hn.dtype), pltpu.SemaphoreType.DMA((2,))],
    )
    return pl.pallas_call(
        _experts_kernel,
        grid_spec=grid_spec,
        out_shape=jax.ShapeDtypeStruct((n_rows, d), F32),
        compiler_params=pltpu.CompilerParams(dimension_semantics=("arbitrary",),
                                             vmem_limit_bytes=VMEM_LIMIT, disable_bounds_checks=True),
        name="experts",
    )(tile_expert, n_used, src, hn, w_g, w_u, w_d)


def _combine_kernel(d1_ref, d2_ref, y_hbm, x1_ref, route_ref, gf_ref, o_ref, ybuf, sem):
    i = pl.program_id(0)
    n = pl.num_programs(0)
    tm = x1_ref.shape[0]
    slot = i % 2

    def gather(tile, s):
        _gather_rows(y_hbm, d1_ref, tile * tm, ybuf.at[s, 0], sem.at[s], tm)
        _gather_rows(y_hbm, d2_ref, tile * tm, ybuf.at[s, 1], sem.at[s], tm)

    @pl.when(i == 0)
    def _():
        gather(0, 0)

    @pl.when(i + 1 < n)
    def _():
        gather(i + 1, 1 - slot)

    _wait_rows(y_hbm, ybuf.at[slot, 0], sem.at[slot])
    _wait_rows(y_hbm, ybuf.at[slot, 1], sem.at[slot])
    route = route_ref[...]
    x2 = x1_ref[...] + route[:, 0:1] * ybuf[slot, 0] + route[:, 1:2] * ybuf[slot, 1]
    o_ref[...] = _rms(x2, gf_ref[...])


def _combine(d1, d2, y, x1, route, g_final, *, tm):
    t_tok, d = x1.shape
    grid_spec = pltpu.PrefetchScalarGridSpec(
        num_scalar_prefetch=2,
        grid=(t_tok // tm,),
        in_specs=[
            pl.BlockSpec(memory_space=pl.ANY),
            pl.BlockSpec((tm, d), lambda i, a, b: (i, 0)),
            pl.BlockSpec((tm, LANES), lambda i, a, b: (i, 0)),
            pl.BlockSpec((1, d), lambda i, a, b: (0, 0)),
        ],
        out_specs=pl.BlockSpec((tm, d), lambda i, a, b: (i, 0)),
        scratch_shapes=[pltpu.VMEM((2, 2, tm, d), F32), pltpu.SemaphoreType.DMA((2,))],
    )
    return pl.pallas_call(
        _combine_kernel,
        grid_spec=grid_spec,
        out_shape=jax.ShapeDtypeStruct((t_tok, d), F32),
        compiler_params=pltpu.CompilerParams(dimension_semantics=("arbitrary",),
                                             vmem_limit_bytes=VMEM_LIMIT, disable_bounds_checks=True),
        name="combine",
    )(d1, d2, y, x1, route, g_final)


def _dispatch_plan(route, counts, *, tm, n_tiles):
    t_tok = route.shape[0]
    cnt = counts[0, :N_EXPERTS].astype(jnp.int32)
    tiles_per_e = (cnt + tm - 1) // tm
    tile_end = jnp.cumsum(tiles_per_e)
    off = (tile_end - tiles_per_e) * tm
    e1 = route[:, 2].astype(jnp.int32)
    e2 = route[:, 3].astype(jnp.int32)
    d1 = off[e1] + route[:, 4].astype(jnp.int32)
    d2 = off[e2] + route[:, 5].astype(jnp.int32)
    tok = jnp.arange(t_tok, dtype=jnp.int32)
    src = jnp.zeros((n_tiles * tm,), jnp.int32).at[jnp.concatenate([d1, d2])].add(
        jnp.concatenate([tok, tok]))
    tile_ids = jnp.arange(n_tiles, dtype=jnp.int32)
    tile_expert = jnp.minimum(
        jnp.sum((tile_end[None, :] <= tile_ids[:, None]).astype(jnp.int32), axis=1), N_EXPERTS - 1)
    return d1, d2, src, tile_expert, tile_end[-1:].astype(jnp.int32)


def _tile(n, pref):
    t = min(n, pref)
    assert n % t == 0, (n, t)
    return t


def kernel(x, positions, g_mix, w_in, w_gate, lambda_params, g_subln, w_br_diff, w_br_sb, w_out,
           g_moe, w_group, b_group, w_router, b_router, w_up_gate, w_up, w_down, g_final):
    b, s_len, d = x.shape
    t_tok = b * s_len
    depth = w_in.shape[0]
    assert depth == 1
    l = 0
    x2 = x.reshape(t_tok, d)

    inv_freq = ROPE_THETA ** (-jnp.arange(0, ROT_DIM, 2, dtype=F32) / ROT_DIM)
    lane = np.arange(LANES)
    invf = inv_freq[(lane % DIFF_HEAD_DIM) % (ROT_DIM // 2)].reshape(1, LANES)
    posf = positions.astype(F32).reshape(t_tok, 1)

    in_w = w_in.shape[2]
    w_cat = jnp.concatenate([w_in[l].astype(BF16), w_gate[l].astype(BF16)], axis=1)
    tn = 1024
    qk_w = N_HEADS_DIFF * 2 * DIFF_HEAD_DIM
    proj = _proj(x2, posf, invf, g_mix[l].reshape(1, d), w_cat,
                 tm=_tile(t_tok, 512), tn=tn,
                 n_rope_tiles=2 * qk_w // tn, n_plain_tiles=(in_w - 2 * qk_w) // tn,
                 sb_q_tile=(2 * qk_w + N_HEADS_DIFF * DIFF_V_DIM) // tn)
    proj3 = proj.reshape(b, s_len, proj.shape[1])

    o_diff = _dattn(proj3, lambda_params[l], g_subln[l].reshape(1, DIFF_V_DIM), tq=_tile(s_len, 512))
    tq = _tile(s_len, 256)
    r = np.arange(tq)
    tri = jnp.asarray(r[:, None] > r[None, :], dtype=BF16)
    sb_col0 = (2 * qk_w + N_HEADS_DIFF * DIFF_V_DIM) // LANES
    o_sb = _sbattn(proj3, tri, tq=tq, col0=sb_col0)

    w_r = jnp.concatenate([w_router[l], w_group[l]], axis=1)
    w_r = jnp.pad(w_r, ((0, 0), (0, LANES - w_r.shape[1])))
    wr_hi = w_r.astype(BF16)
    wr_lo = (w_r - wr_hi.astype(F32)).astype(BF16)
    b_r = jnp.pad(jnp.concatenate([b_router[l], b_group[l]]), (0, LANES - N_EXPERTS - N_GROUPS))
    x1, hn, route, counts = _mix(
        o_diff.reshape(t_tok, -1), o_sb.reshape(t_tok, -1), proj, x2,
        w_br_diff[l].astype(BF16), w_br_sb[l].astype(BF16), w_out[l].astype(BF16),
        g_moe[l].reshape(1, d), wr_hi, wr_lo, b_r.reshape(1, LANES),
        tm=_tile(t_tok, 256), gate_col0=in_w)

    tm_e = _tile(t_tok, 512)
    n_tiles = 2 * t_tok // tm_e + N_EXPERTS
    d1, d2, src, tile_expert, n_used = _dispatch_plan(route, counts, tm=tm_e, n_tiles=n_tiles)
    y = _experts(tile_expert, n_used, src, hn,
                 w_up_gate[l].astype(BF16), w_up[l].astype(BF16), w_down[l].astype(BF16), tm=tm_e)
    out = _combine(d1, d2, y, x1, route, g_final.reshape(1, d), tm=_tile(t_tok, 256))
    return out.reshape(b, s_len, d)
```

```python
import functools

import numpy as np
import jax
import jax.numpy as jnp
from jax import lax
from jax.experimental import pallas as pl
from jax.experimental.pallas import tpu as pltpu

F32 = jnp.float32
BF16 = jnp.bfloat16

N_HEADS_DIFF = 8
DIFF_HEAD_DIM = 64
DIFF_V_DIM = 128
N_HEADS_SB = 8
SB_HEAD_DIM = 128
ROT_DIM = 16
ROPE_THETA = 500000.0
N_GROUPS = 4
EXPERTS_PER_GROUP = 8
N_EXPERTS = 32
EPS = 1e-6
NEG_INF = float(np.finfo(np.float32).min)
SB_CUTOFF = -150.5
LAMBDA_INIT = 0.8 - 0.6 * float(np.exp(-0.3 * 0))
LOG2E = float(np.log2(np.e))

LANES = 128
MXU_N = 256
VMEM_LIMIT = 56 * 1024 * 1024


def _cparams(sem):
    return pltpu.CompilerParams(dimension_semantics=sem, vmem_limit_bytes=VMEM_LIMIT)


def _rms(x, g):
    return x * lax.rsqrt(jnp.mean(x * x, axis=-1, keepdims=True) + EPS) * g


def _proj_kernel(x_ref, pos_ref, invf_ref, g_ref, w_ref, o_ref, h_ref, c_ref, s1_ref, s2_ref,
                 *, n_rope_tiles, n_plain_tiles, sb_q_tile):
    j = pl.program_id(1)
    tn = o_ref.shape[1]

    @pl.when(j == 0)
    def _():
        h_ref[...] = _rms(x_ref[...], g_ref[...]).astype(BF16)
        ang = pos_ref[...] * invf_ref[...]
        lane = lax.broadcasted_iota(jnp.int32, ang.shape, 1) % DIFF_HEAD_DIM
        cos = jnp.cos(ang)
        sin = jnp.sin(ang)
        half = ROT_DIM // 2
        c_ref[...] = jnp.where(lane < ROT_DIM, cos, 1.0)
        s1_ref[...] = jnp.where(lane < half, -sin, 0.0)
        s2_ref[...] = jnp.where((lane >= half) & (lane < ROT_DIM), sin, 0.0)

    def chunks():
        h = h_ref[...]
        for c in range(tn // MXU_N):
            cols = slice(c * MXU_N, (c + 1) * MXU_N)
            yield cols, jnp.dot(h, w_ref[:, cols], preferred_element_type=F32)

    @pl.when(j < n_rope_tiles)
    def _():
        half = ROT_DIM // 2
        qscale = jnp.where(j == 0, DIFF_HEAD_DIM ** -0.5 * LOG2E, 1.0).astype(F32)
        for cols, acc in chunks():
            parts = []
            for c in range(MXU_N // LANES):
                t = acc[:, c * LANES:(c + 1) * LANES]
                up = pltpu.roll(t, LANES - half, axis=1)
                dn = pltpu.roll(t, half, axis=1)
                parts.append((t * c_ref[...] + up * s1_ref[...] + dn * s2_ref[...]) * qscale)
            o_ref[:, cols] = jnp.concatenate(parts, axis=1).astype(o_ref.dtype)

    @pl.when((j >= n_rope_tiles) & (j < n_rope_tiles + n_plain_tiles))
    def _():
        scale = jnp.where(j == sb_q_tile, SB_HEAD_DIM ** -0.5 * LOG2E, 1.0).astype(F32)
        for cols, acc in chunks():
            o_ref[:, cols] = (acc * scale).astype(o_ref.dtype)

    @pl.when(j >= n_rope_tiles + n_plain_tiles)
    def _():
        for cols, acc in chunks():
            o_ref[:, cols] = jax.nn.sigmoid(acc).astype(o_ref.dtype)


def _proj(x2, posf, invf, g_mix, w_cat, *, tm, tn, n_rope_tiles, n_plain_tiles, sb_q_tile):
    t_tok, d = x2.shape
    n = w_cat.shape[1]
    kern = functools.partial(_proj_kernel, n_rope_tiles=n_rope_tiles, n_plain_tiles=n_plain_tiles,
                             sb_q_tile=sb_q_tile)
    return pl.pallas_call(
        kern,
        grid=(t_tok // tm, n // tn),
        in_specs=[
            pl.BlockSpec((tm, d), lambda i, j: (i, 0)),
            pl.BlockSpec((tm, 1), lambda i, j: (i, 0)),
            pl.BlockSpec((1, LANES), lambda i, j: (0, 0)),
            pl.BlockSpec((1, d), lambda i, j: (0, 0)),
            pl.BlockSpec((d, tn), lambda i, j: (0, j)),
        ],
        out_specs=pl.BlockSpec((tm, tn), lambda i, j: (i, j)),
        out_shape=jax.ShapeDtypeStruct((t_tok, n), BF16),
        scratch_shapes=[
            pltpu.VMEM((tm, d), BF16),
            pltpu.VMEM((tm, LANES), F32),
            pltpu.VMEM((tm, LANES), F32),
            pltpu.VMEM((tm, LANES), F32),
        ],
        compiler_params=_cparams(("parallel", "arbitrary")),
        name="proj",
    )(x2, posf, invf, g_mix, w_cat)


def _dattn_kernel(q_ref, k_ref, v_ref, lp_ref, g_ref, o_ref, s_ref, ml_ref, mb_ref, acc_ref, vx_ref, *, tq):
    s_len = q_ref.shape[1]
    n_heads = q_ref.shape[2] // LANES
    nq = s_len // tq
    nc = tq // LANES
    lp = lp_ref[...]
    lam = (jnp.exp(jnp.sum(lp[0:1] * lp[1:2], axis=-1, keepdims=True))
           - jnp.exp(jnp.sum(lp[2:3] * lp[3:4], axis=-1, keepdims=True)) + LAMBDA_INIT)
    lane = lax.broadcasted_iota(jnp.int32, (tq, LANES), 1)
    row = lax.broadcasted_iota(jnp.int32, (2 * tq, tq), 0) % tq
    col = lax.broadcasted_iota(jnp.int32, (2 * tq, tq), 1)
    causal = col <= row
    heads = [(hd, slice(hd * LANES, (hd + 1) * LANES)) for hd in range(n_heads)]
    for hd, cols in heads:
        vx_ref[hd, :, :DIFF_V_DIM] = v_ref[0, :, cols]
        vx_ref[hd, :, DIFF_V_DIM:] = jnp.ones((s_len, LANES), vx_ref.dtype)

    def rows(i):
        return pl.ds(pl.multiple_of(i * tq, tq), tq)

    def q_body(qi, carry):
        qqs = []
        for hd, cols in heads:
            q = q_ref[0, rows(qi), cols]
            zero = jnp.zeros_like(q)
            qqs.append(jnp.concatenate([jnp.where(lane < DIFF_HEAD_DIM, q, zero),
                                        jnp.where(lane >= DIFF_HEAD_DIM, q, zero)], axis=0))
        ml_ref[...] = jnp.full(ml_ref.shape, NEG_INF, F32)

        def scores(kj, masked):
            for hd, cols in heads:
                k = k_ref[0, rows(kj), cols]
                s = lax.dot_general(qqs[hd], k, (((1,), (1,)), ((), ())), preferred_element_type=F32)
                if masked:
                    s = jnp.where(causal, s, NEG_INF)
                s_ref[hd, kj] = s
                m = ml_ref[hd]
                for c in range(nc):
                    m = jnp.maximum(m, s[:, c * LANES:(c + 1) * LANES])
                ml_ref[hd] = m

        def s_body(kj, c):
            scores(kj, False)
            return c

        lax.fori_loop(0, qi, s_body, 0)
        scores(qi, True)

        for hd, _ in heads:
            mb_ref[hd] = jnp.broadcast_to(jnp.max(ml_ref[hd], axis=-1, keepdims=True), mb_ref.shape[1:])
        acc_ref[...] = jnp.zeros(acc_ref.shape, F32)

        def av_body(kj, c):
            for hd, _ in heads:
                mb = mb_ref[hd]
                p = jnp.concatenate(
                    [jnp.exp2(s_ref[hd, kj, :, c * LANES:(c + 1) * LANES] - mb) for c in range(nc)], axis=1)
                acc_ref[hd] += jnp.dot(p.astype(BF16), vx_ref[hd, rows(kj), :], preferred_element_type=F32)
            return c

        lax.fori_loop(0, qi + 1, av_body, 0)

        for hd, cols in heads:
            o = acc_ref[hd, :, :DIFF_V_DIM] / acc_ref[hd, :, DIFF_V_DIM:]
            od = o[:tq] - lam * o[tq:]
            od = _rms(od, g_ref[...]) * (1.0 - LAMBDA_INIT)
            o_ref[0, rows(qi), cols] = od.astype(o_ref.dtype)
        return carry

    lax.fori_loop(0, nq, q_body, 0)


def _dattn(proj3, lam_params, g_subln, *, tq, heads_per_step):
    b, s_len, _ = proj3.shape
    h = N_HEADS_DIFF
    hp = heads_per_step
    kern = functools.partial(_dattn_kernel, tq=tq)
    blk = lambda off: pl.BlockSpec((1, s_len, hp * LANES), lambda bi, hi: (bi, 0, off // hp + hi))
    return pl.pallas_call(
        kern,
        grid=(b, h // hp),
        in_specs=[
            blk(0), blk(h), blk(2 * h),
            pl.BlockSpec(lam_params.shape, lambda bi, hi: (0, 0)),
            pl.BlockSpec((1, DIFF_V_DIM), lambda bi, hi: (0, 0)),
        ],
        out_specs=pl.BlockSpec((1, s_len, hp * LANES), lambda bi, hi: (bi, 0, hi)),
        out_shape=jax.ShapeDtypeStruct((b, s_len, h * DIFF_V_DIM), BF16),
        scratch_shapes=[
            pltpu.VMEM((hp, s_len // tq, 2 * tq, tq), F32),
            pltpu.VMEM((hp, 2 * tq, LANES), F32),
            pltpu.VMEM((hp, 2 * tq, LANES), F32),
            pltpu.VMEM((hp, 2 * tq, DIFF_V_DIM + LANES), F32),
            pltpu.VMEM((hp, s_len, DIFF_V_DIM + LANES), BF16),
        ],
        compiler_params=_cparams(("parallel", "parallel")),
        name="dattn",
    )(proj3, proj3, proj3, lam_params, g_subln)


def _sb_terms(q, k, u, strict):
    z = lax.dot_general(q, k, (((1,), (1,)), ((), ())), preferred_element_type=F32)
    l1p = jnp.log2(1.0 + jnp.exp2(-jnp.abs(z)))
    lom = jnp.minimum(-z, 0.0) - l1p
    log_beta = lom + z
    if strict is not None:
        lom = jnp.where(strict, lom, 0.0)
    hi = lom.astype(BF16)
    lo = (lom - hi.astype(F32)).astype(BF16)
    suffix = jnp.dot(hi, u, preferred_element_type=F32) + jnp.dot(lo, u, preferred_element_type=F32)
    return log_beta, lom, suffix


def _lane_rowsum(x):
    return jnp.broadcast_to(jnp.sum(x, axis=-1, keepdims=True), (x.shape[0], LANES))


def _exp_shifted(x, rb):
    nc = x.shape[1] // LANES
    return jnp.concatenate([jnp.exp2(x[:, c * LANES:(c + 1) * LANES] + rb) for c in range(nc)], axis=1)


def _sbattn_kernel(q_ref, k_ref, v_ref, u_ref, o_ref, rb_ref, acc_ref, *, tq):
    s_len = q_ref.shape[1]
    n_heads = q_ref.shape[2] // LANES
    nq = s_len // tq
    row = lax.broadcasted_iota(jnp.int32, (tq, tq), 0)
    col = lax.broadcasted_iota(jnp.int32, (tq, tq), 1)
    strict = col < row
    heads = [(hd, slice(hd * LANES, (hd + 1) * LANES)) for hd in range(n_heads)]

    def rows(i):
        return pl.ds(pl.multiple_of(i * tq, tq), tq)

    def q_tile(qi, first):
        u = u_ref[...]
        rmax = None
        for hd, cols in heads:
            q = q_ref[0, rows(qi), cols]
            log_beta, lom, suffix = _sb_terms(q, k_ref[0, rows(qi), cols], u, strict)
            a = jnp.where(strict, jnp.exp2(log_beta + suffix), 0.0)
            acc = jnp.dot(a.astype(BF16), v_ref[0, rows(qi), cols], preferred_element_type=F32)
            rb = _lane_rowsum(lom)
            if first:
                o_ref[0, rows(qi), cols] = acc.astype(o_ref.dtype)
                continue
            log_beta, lom, suffix = _sb_terms(q, k_ref[0, rows(qi - 1), cols], u, None)
            a = _exp_shifted(log_beta + suffix, rb)
            acc_ref[hd] = acc + jnp.dot(a.astype(BF16), v_ref[0, rows(qi - 1), cols],
                                        preferred_element_type=F32)
            rb = rb + _lane_rowsum(lom)
            rb_ref[hd] = rb
            rmax = jnp.max(rb) if rmax is None else jnp.maximum(rmax, jnp.max(rb))
        if first:
            return

        def cond(c):
            n, rmax = c
            return (n <= qi) & (rmax > SB_CUTOFF)

        def body(c):
            n, _ = c
            kj = qi - n
            rmax = None
            for hd, cols in heads:
                log_beta, lom, suffix = _sb_terms(q_ref[0, rows(qi), cols], k_ref[0, rows(kj), cols],
                                                  u_ref[...], None)
                a = _exp_shifted(log_beta + suffix, rb_ref[hd])
                acc_ref[hd] += jnp.dot(a.astype(BF16), v_ref[0, rows(kj), cols], preferred_element_type=F32)
                rb_new = rb_ref[hd] + _lane_rowsum(lom)
                rb_ref[hd] = rb_new
                rmax = jnp.max(rb_new) if rmax is None else jnp.maximum(rmax, jnp.max(rb_new))
            return n + 1, rmax

        lax.while_loop(cond, body, (jnp.int32(2), rmax))
        for hd, cols in heads:
            o_ref[0, rows(qi), cols] = acc_ref[hd].astype(o_ref.dtype)

    q_tile(0, True)

    def q_body(qi, carry):
        q_tile(qi, False)
        return carry

    lax.fori_loop(1, nq, q_body, 0)


def _sbattn(proj3, tri, *, tq, col0, heads_per_step):
    b, s_len, _ = proj3.shape
    h = N_HEADS_SB
    hp = heads_per_step
    kern = functools.partial(_sbattn_kernel, tq=tq)
    blk = lambda off: pl.BlockSpec((1, s_len, hp * LANES), lambda bi, hi: (bi, 0, off // hp + hi))
    return pl.pallas_call(
        kern,
        grid=(b, h // hp),
        in_specs=[
            blk(col0), blk(col0 + h), blk(col0 + 2 * h),
            pl.BlockSpec((tq, tq), lambda bi, hi: (0, 0)),
        ],
        out_specs=pl.BlockSpec((1, s_len, hp * LANES), lambda bi, hi: (bi, 0, hi)),
        out_shape=jax.ShapeDtypeStruct((b, s_len, h * SB_HEAD_DIM), BF16),
        scratch_shapes=[
            pltpu.VMEM((hp, tq, LANES), F32),
            pltpu.VMEM((hp, tq, SB_HEAD_DIM), F32),
        ],
        compiler_params=_cparams(("parallel", "parallel")),
        name="sbattn",
    )(proj3, proj3, proj3, tri)


def _split_bf16(a):
    hi = a.astype(BF16)
    return hi, (a - hi.astype(F32)).astype(BF16)


def _mix_kernel(od_ref, os_ref, g0_ref, g1_ref, x_ref, wd_ref, ws_ref, wo_ref, gm_ref,
                wr_hi_ref, wr_lo_ref, br_ref, ltri_ref, x1_ref, hn_ref, route_ref, cnt_ref):
    @pl.when(pl.program_id(0) == 0)
    def _():
        cnt_ref[...] = jnp.zeros(cnt_ref.shape, F32)

    yd = jnp.dot(od_ref[...], wd_ref[...], preferred_element_type=F32)
    ys = jnp.dot(os_ref[...], ws_ref[...], preferred_element_type=F32)
    mixed = g0_ref[...].astype(F32) * yd + g1_ref[...].astype(F32) * ys
    x1 = x_ref[...] + jnp.dot(mixed.astype(BF16), wo_ref[...], preferred_element_type=F32)
    x1_ref[...] = x1
    hn = _rms(x1, gm_ref[...])
    hn_hi, hn_lo = _split_bf16(hn)
    hn_ref[...] = hn_hi.astype(hn_ref.dtype)

    tm = hn.shape[0]
    quad = jnp.dot(jnp.concatenate([hn_hi, hn_lo], axis=0),
                   jnp.concatenate([wr_hi_ref[...], wr_lo_ref[...]], axis=1), preferred_element_type=F32)
    logits = (quad[:tm, :LANES] + quad[:tm, LANES:]) + (quad[tm:, :LANES] + quad[tm:, LANES:]) + br_ref[...]
    lane = lax.broadcasted_iota(jnp.int32, logits.shape, 1)
    big = jnp.int32(LANES)

    def first_argmax(vals, vmax):
        return jnp.min(jnp.where(vals == vmax, lane, big), axis=-1, keepdims=True)

    is_group = (lane >= N_EXPERTS) & (lane < N_EXPERTS + N_GROUPS)
    gl = jnp.where(is_group, logits, -jnp.inf)
    gmax = jnp.max(gl, axis=-1, keepdims=True)
    gexp = jnp.exp(gl - gmax)
    probs = gexp / jnp.sum(gexp, axis=-1, keepdims=True)
    g_val = jnp.max(probs, axis=-1, keepdims=True)
    g_idx = first_argmax(jnp.where(is_group, probs, -jnp.inf), g_val) - N_EXPERTS

    in_group = (lane >= g_idx * EXPERTS_PER_GROUP) & (lane < (g_idx + 1) * EXPERTS_PER_GROUP)
    el = jnp.where(in_group, logits, -jnp.inf)
    v1 = jnp.max(el, axis=-1, keepdims=True)
    i1 = first_argmax(el, v1)
    el2 = jnp.where(lane == i1, -jnp.inf, el)
    v2 = jnp.max(el2, axis=-1, keepdims=True)
    i2 = first_argmax(el2, v2)
    e2 = jnp.exp(v2 - v1)
    den = 1.0 + e2
    w1 = (1.0 / den) * g_val
    w2 = (e2 / den) * g_val

    pick1 = lane == i1
    pick2 = lane == i2
    onehot = jnp.where(pick1 | pick2, 1.0, 0.0).astype(BF16)
    seen = jnp.dot(ltri_ref[...], onehot, preferred_element_type=F32) + cnt_ref[...]
    rank1 = jnp.sum(jnp.where(pick1, seen, 0.0), axis=-1, keepdims=True) - 1.0
    rank2 = jnp.sum(jnp.where(pick2, seen, 0.0), axis=-1, keepdims=True) - 1.0
    cnt_ref[...] = seen[seen.shape[0] - 1:, :]
    cols = (w1, w2, i1.astype(F32), i2.astype(F32), rank1, rank2)
    route = jnp.zeros(logits.shape, F32)
    for c, val in enumerate(cols):
        route = jnp.where(lane == c, val, route)
    route_ref[...] = route


def _mix(o_diff, o_sb, proj, x2, w_brd, w_brs, w_out, g_moe, wr_hi, wr_lo, b_r, *, tm, gate_col0):
    t_tok, d = x2.shape
    wd_in = o_diff.shape[1]
    ws_in = o_sb.shape[1]
    gblk = gate_col0 // d
    r = np.arange(tm)
    ltri = jnp.asarray(r[None, :] <= r[:, None], dtype=BF16)
    const = lambda shape: pl.BlockSpec(shape, lambda i: (0, 0), pipeline_mode=pl.Buffered(1))
    return pl.pallas_call(
        _mix_kernel,
        grid=(t_tok // tm,),
        in_specs=[
            pl.BlockSpec((tm, wd_in), lambda i: (i, 0)),
            pl.BlockSpec((tm, ws_in), lambda i: (i, 0)),
            pl.BlockSpec((tm, d), lambda i: (i, gblk)),
            pl.BlockSpec((tm, d), lambda i: (i, gblk + 1)),
            pl.BlockSpec((tm, d), lambda i: (i, 0)),
            const(w_brd.shape), const(w_brs.shape), const(w_out.shape),
            const(g_moe.shape), const(wr_hi.shape), const(wr_lo.shape), const(b_r.shape),
            const(ltri.shape),
        ],
        out_specs=[
            pl.BlockSpec((tm, d), lambda i: (i, 0)),
            pl.BlockSpec((tm, d), lambda i: (i, 0)),
            pl.BlockSpec((tm, LANES), lambda i: (i, 0)),
            pl.BlockSpec((1, LANES), lambda i: (0, 0)),
        ],
        out_shape=[
            jax.ShapeDtypeStruct((t_tok, d), F32),
            jax.ShapeDtypeStruct((t_tok, d), F32),
            jax.ShapeDtypeStruct((t_tok, LANES), F32),
            jax.ShapeDtypeStruct((1, LANES), F32),
        ],
        compiler_params=_cparams(("arbitrary",)),
        name="mix",
    )(o_diff, o_sb, proj, proj, x2, w_brd, w_brs, w_out, g_moe, wr_hi, wr_lo, b_r, ltri)


def _gather_rows(src_hbm, idx_ref, base, dst, sem, n_rows, first_row=0):
    for r in range(first_row, n_rows):
        row = idx_ref[base + r]
        pltpu.make_async_copy(src_hbm.at[pl.ds(row, 1), :], dst.at[pl.ds(r, 1), :], sem).start()


def _wait_rows(src_hbm, dst, sem):
    pltpu.make_async_copy(src_hbm.at[pl.ds(0, dst.shape[0]), :], dst, sem).wait()


def _experts_kernel(te_ref, nused_ref, src_ref, hn_hbm, wg_ref, wu_ref, wd_ref, y_ref, xbuf, xb_ref, sem):
    i = pl.program_id(0)
    tm = xbuf.shape[1]
    n_used = nused_ref[0]
    slot = i % 2

    @pl.when(i == 0)
    def _():
        _gather_rows(hn_hbm, src_ref, 0, xbuf.at[0], sem.at[0], tm)

    @pl.when(i < n_used)
    def _():
        _wait_rows(hn_hbm, xbuf.at[slot], sem.at[slot])
        xb_ref[...] = xbuf[slot].astype(BF16)
        nxt = jnp.minimum(i + 1, n_used - 1)
        ff = wg_ref.shape[2]
        d = wd_ref.shape[2]
        n_pieces = 2 * (ff // MXU_N) + d // MXU_N
        per_piece = tm // n_pieces
        piece = [0]

        def gather_piece():
            r0 = piece[0] * per_piece
            r1 = tm if piece[0] == n_pieces - 1 else r0 + per_piece
            _gather_rows(hn_hbm, src_ref, nxt * tm, xbuf.at[1 - slot], sem.at[1 - slot], r1, r0)
            piece[0] += 1

        x = xb_ref[...]
        acts = []
        for c in range(ff // MXU_N):
            cols = slice(c * MXU_N, (c + 1) * MXU_N)
            gather_piece()
            g = jnp.dot(x, wg_ref[0, :, cols], preferred_element_type=F32)
            gather_piece()
            u = jnp.dot(x, wu_ref[0, :, cols], preferred_element_type=F32)
            acts.append((jax.nn.silu(g) * u).astype(BF16))
        a = jnp.concatenate(acts, axis=1)
        for c in range(d // MXU_N):
            cols = slice(c * MXU_N, (c + 1) * MXU_N)
            gather_piece()
            y_ref[:, cols] = jnp.dot(a, wd_ref[0, :, cols], preferred_element_type=F32)

    @pl.when(i == n_used - 1)
    def _():
        _wait_rows(hn_hbm, xbuf.at[1 - slot], sem.at[1 - slot])

    @pl.when(i >= n_used)
    def _():
        y_ref[...] = jnp.zeros(y_ref.shape, y_ref.dtype)


def _experts(tile_expert, n_used, src, hn, w_g, w_u, w_d, *, tm):
    n_rows = src.shape[0]
    d = hn.shape[1]
    n_e, _, ff = w_g.shape
    grid_spec = pltpu.PrefetchScalarGridSpec(
        num_scalar_prefetch=3,
        grid=(n_rows // tm,),
        in_specs=[
            pl.BlockSpec(memory_space=pl.ANY),
            pl.BlockSpec((1, d, ff), lambda i, te, nu, sr: (te[i], 0, 0)),
            pl.BlockSpec((1, d, ff), lambda i, te, nu, sr: (te[i], 0, 0)),
            pl.BlockSpec((1, ff, d), lambda i, te, nu, sr: (te[i], 0, 0)),
        ],
        out_specs=pl.BlockSpec((tm, d), lambda i, te, nu, sr: (i, 0)),
        scratch_shapes=[pltpu.VMEM((2, tm, hn.shape[1]), hn.dtype), pltpu.VMEM((tm, d), BF16),
                        pltpu.SemaphoreType.DMA((2,))],
    )
    return pl.pallas_call(
        _experts_kernel,
        grid_spec=grid_spec,
        out_shape=jax.ShapeDtypeStruct((n_rows, d), F32),
        compiler_params=pltpu.CompilerParams(dimension_semantics=("arbitrary",),
                                             vmem_limit_bytes=VMEM_LIMIT, disable_bounds_checks=True),
        name="experts",
    )(tile_expert, n_used, src, hn, w_g, w_u, w_d)


def _combine_kernel(d1_ref, d2_ref, y_hbm, x1_ref, route_ref, gf_ref, o_ref, ybuf, sem):
    i = pl.program_id(0)
    n = pl.num_programs(0)
    tm = x1_ref.shape[0]
    slot = i % 2

    def gather(tile, s):
        _gather_rows(y_hbm, d1_ref, tile * tm, ybuf.at[s, 0], sem.at[s], tm)
        _gather_rows(y_hbm, d2_ref, tile * tm, ybuf.at[s, 1], sem.at[s], tm)

    @pl.when(i == 0)
    def _():
        gather(0, 0)

    @pl.when(i + 1 < n)
    def _():
        gather(i + 1, 1 - slot)

    _wait_rows(y_hbm, ybuf.at[slot, 0], sem.at[slot])
    _wait_rows(y_hbm, ybuf.at[slot, 1], sem.at[slot])
    route = route_ref[...]
    x2 = x1_ref[...] + route[:, 0:1] * ybuf[slot, 0] + route[:, 1:2] * ybuf[slot, 1]
    o_ref[...] = _rms(x2, gf_ref[...])


def _combine(d1, d2, y, x1, route, g_final, *, tm):
    t_tok, d = x1.shape
    grid_spec = pltpu.PrefetchScalarGridSpec(
        num_scalar_prefetch=2,
        grid=(t_tok // tm,),
        in_specs=[
            pl.BlockSpec(memory_space=pl.ANY),
            pl.BlockSpec((tm, d), lambda i, a, b: (i, 0)),
            pl.BlockSpec((tm, LANES), lambda i, a, b: (i, 0)),
            pl.BlockSpec((1, d), lambda i, a, b: (0, 0)),
        ],
        out_specs=pl.BlockSpec((tm, d), lambda i, a, b: (i, 0)),
        scratch_shapes=[pltpu.VMEM((2, 2, tm, d), F32), pltpu.SemaphoreType.DMA((2,))],
    )
    return pl.pallas_call(
        _combine_kernel,
        grid_spec=grid_spec,
        out_shape=jax.ShapeDtypeStruct((t_tok, d), F32),
        compiler_params=pltpu.CompilerParams(dimension_semantics=("arbitrary",),
                                             vmem_limit_bytes=VMEM_LIMIT, disable_bounds_checks=True),
        name="combine",
    )(d1, d2, y, x1, route, g_final)


def _pair_rows_kernel(route_ref, off_ref, d_ref):
    route = route_ref[...]
    off = off_ref[...]
    lane = lax.broadcasted_iota(jnp.int32, route.shape, 1)
    lane_f = lane.astype(F32)

    def row_of(e_col, rank_col):
        first = jnp.sum(jnp.where(lane_f == route[:, e_col:e_col + 1], off, 0.0), axis=-1, keepdims=True)
        return first + route[:, rank_col:rank_col + 1]

    d = jnp.where(lane == 0, row_of(2, 4), jnp.where(lane == 1, row_of(3, 5), 0.0))
    d_ref[...] = d.astype(jnp.int32)


def _pair_rows(route, off_row, *, tm):
    t_tok = route.shape[0]
    return pl.pallas_call(
        _pair_rows_kernel,
        grid=(t_tok // tm,),
        in_specs=[pl.BlockSpec((tm, LANES), lambda i: (i, 0)), pl.BlockSpec((1, LANES), lambda i: (0, 0))],
        out_specs=pl.BlockSpec((tm, LANES), lambda i: (i, 0)),
        out_shape=jax.ShapeDtypeStruct((t_tok, LANES), jnp.int32),
        compiler_params=_cparams(("parallel",)),
        name="pair_rows",
    )(route, off_row)


def _dispatch_plan(route, counts, *, tm, n_tiles):
    t_tok = route.shape[0]
    cnt = counts[0, :N_EXPERTS].astype(jnp.int32)
    tiles_per_e = (cnt + tm - 1) // tm
    tile_end = jnp.cumsum(tiles_per_e)
    off = (tile_end - tiles_per_e) * tm
    off_row = jnp.pad(off.astype(F32), (0, LANES - N_EXPERTS)).reshape(1, LANES)
    pair_rows = _pair_rows(route, off_row, tm=_tile(t_tok, 2048))
    d1 = pair_rows[:, 0]
    d2 = pair_rows[:, 1]
    tok = jnp.arange(t_tok, dtype=jnp.int32)
    src = jnp.zeros((n_tiles * tm,), jnp.int32).at[jnp.concatenate([d1, d2])].add(
        jnp.concatenate([tok, tok]))
    tile_ids = jnp.arange(n_tiles, dtype=jnp.int32)
    tile_expert = jnp.minimum(
        jnp.sum((tile_end[None, :] <= tile_ids[:, None]).astype(jnp.int32), axis=1), N_EXPERTS - 1)
    return d1, d2, src, tile_expert, tile_end[-1:].astype(jnp.int32)


def _tile(n, pref):
    t = min(n, pref)
    assert n % t == 0, (n, t)
    return t


def kernel(x, positions, g_mix, w_in, w_gate, lambda_params, g_subln, w_br_diff, w_br_sb, w_out,
           g_moe, w_group, b_group, w_router, b_router, w_up_gate, w_up, w_down, g_final):
    b, s_len, d = x.shape
    t_tok = b * s_len
    depth = w_in.shape[0]
    assert depth == 1
    l = 0
    x2 = x.reshape(t_tok, d)

    inv_freq = ROPE_THETA ** (-jnp.arange(0, ROT_DIM, 2, dtype=F32) / ROT_DIM)
    lane = np.arange(LANES)
    invf = inv_freq[(lane % DIFF_HEAD_DIM) % (ROT_DIM // 2)].reshape(1, LANES)
    posf = positions.astype(F32).reshape(t_tok, 1)

    in_w = w_in.shape[2]
    w_cat = jnp.concatenate([w_in[l].astype(BF16), w_gate[l].astype(BF16)], axis=1)
    tn = 1024
    qk_w = N_HEADS_DIFF * 2 * DIFF_HEAD_DIM
    proj = _proj(x2, posf, invf, g_mix[l].reshape(1, d), w_cat,
                 tm=_tile(t_tok, 512), tn=tn,
                 n_rope_tiles=2 * qk_w // tn, n_plain_tiles=(in_w - 2 * qk_w) // tn,
                 sb_q_tile=(2 * qk_w + N_HEADS_DIFF * DIFF_V_DIM) // tn)
    proj3 = proj.reshape(b, s_len, proj.shape[1])

    o_diff = _dattn(proj3, lambda_params[l], g_subln[l].reshape(1, DIFF_V_DIM), tq=_tile(s_len, 512),
                    heads_per_step=2)
    tq = _tile(s_len, 256)
    r = np.arange(tq)
    tri = jnp.asarray(r[:, None] > r[None, :], dtype=BF16)
    sb_col0 = (2 * qk_w + N_HEADS_DIFF * DIFF_V_DIM) // LANES
    o_sb = _sbattn(proj3, tri, tq=tq, col0=sb_col0, heads_per_step=4)

    w_r = jnp.concatenate([w_router[l], w_group[l]], axis=1)
    w_r = jnp.pad(w_r, ((0, 0), (0, LANES - w_r.shape[1])))
    wr_hi = w_r.astype(BF16)
    wr_lo = (w_r - wr_hi.astype(F32)).astype(BF16)
    b_r = jnp.pad(jnp.concatenate([b_router[l], b_group[l]]), (0, LANES - N_EXPERTS - N_GROUPS))
    x1, hn, route, counts = _mix(
        o_diff.reshape(t_tok, -1), o_sb.reshape(t_tok, -1), proj, x2,
        w_br_diff[l].astype(BF16), w_br_sb[l].astype(BF16), w_out[l].astype(BF16),
        g_moe[l].reshape(1, d), wr_hi, wr_lo, b_r.reshape(1, LANES),
        tm=_tile(t_tok, 256), gate_col0=in_w)

    tm_e = _tile(t_tok, 512)
    n_tiles = 2 * t_tok // tm_e + N_EXPERTS
    d1, d2, src, tile_expert, n_used = _dispatch_plan(route, counts, tm=tm_e, n_tiles=n_tiles)
    y = _experts(tile_expert, n_used, src, hn,
                 w_up_gate[l].astype(BF16), w_up[l].astype(BF16), w_down[l].astype(BF16), tm=tm_e)
    out = _combine(d1, d2, y, x1, route, g_final.reshape(1, d), tm=_tile(t_tok, 256))
    return out.reshape(b, s_len, d)
```

```python
import functools

import numpy as np
import jax
import jax.numpy as jnp
from jax import lax
from jax.experimental import pallas as pl
from jax.experimental.pallas import tpu as pltpu

F32 = jnp.float32
BF16 = jnp.bfloat16

N_HEADS_DIFF = 8
DIFF_HEAD_DIM = 64
DIFF_V_DIM = 128
N_HEADS_SB = 8
SB_HEAD_DIM = 128
ROT_DIM = 16
ROPE_THETA = 500000.0
N_GROUPS = 4
EXPERTS_PER_GROUP = 8
N_EXPERTS = 32
EPS = 1e-6
NEG_INF = float(np.finfo(np.float32).min)
SB_CUTOFF = -150.5
LAMBDA_INIT = 0.8 - 0.6 * float(np.exp(-0.3 * 0))
LOG2E = float(np.log2(np.e))

LANES = 128
MXU_N = 256
SLAB_ROWS = 16
SLAB_PITCH = 20
VMEM_LIMIT = 56 * 1024 * 1024


def _cparams(sem):
    return pltpu.CompilerParams(dimension_semantics=sem, vmem_limit_bytes=VMEM_LIMIT)


def _rms(x, g):
    return x * lax.rsqrt(jnp.mean(x * x, axis=-1, keepdims=True) + EPS) * g


def _proj_kernel(x_ref, pos_ref, invf_ref, g_ref, w_ref, o_ref, h_ref, c_ref, s1_ref, s2_ref,
                 *, n_rope_tiles, n_plain_tiles, sb_q_tile):
    j = pl.program_id(1)
    tn = o_ref.shape[1]

    @pl.when(j == 0)
    def _():
        h_ref[...] = _rms(x_ref[...], g_ref[...]).astype(BF16)
        ang = pos_ref[...] * invf_ref[...]
        lane = lax.broadcasted_iota(jnp.int32, ang.shape, 1) % DIFF_HEAD_DIM
        cos = jnp.cos(ang)
        sin = jnp.sin(ang)
        half = ROT_DIM // 2
        c_ref[...] = jnp.where(lane < ROT_DIM, cos, 1.0)
        s1_ref[...] = jnp.where(lane < half, -sin, 0.0)
        s2_ref[...] = jnp.where((lane >= half) & (lane < ROT_DIM), sin, 0.0)

    def chunks():
        h = h_ref[...]
        for c in range(tn // MXU_N):
            cols = slice(c * MXU_N, (c + 1) * MXU_N)
            yield cols, jnp.dot(h, w_ref[:, cols], preferred_element_type=F32)

    @pl.when(j < n_rope_tiles)
    def _():
        half = ROT_DIM // 2
        qscale = jnp.where(j == 0, DIFF_HEAD_DIM ** -0.5 * LOG2E, 1.0).astype(F32)
        for cols, acc in chunks():
            parts = []
            for c in range(MXU_N // LANES):
                t = acc[:, c * LANES:(c + 1) * LANES]
                up = pltpu.roll(t, LANES - half, axis=1)
                dn = pltpu.roll(t, half, axis=1)
                parts.append((t * c_ref[...] + up * s1_ref[...] + dn * s2_ref[...]) * qscale)
            o_ref[:, cols] = jnp.concatenate(parts, axis=1).astype(o_ref.dtype)

    @pl.when((j >= n_rope_tiles) & (j < n_rope_tiles + n_plain_tiles))
    def _():
        scale = jnp.where(j == sb_q_tile, SB_HEAD_DIM ** -0.5 * LOG2E, 1.0).astype(F32)
        for cols, acc in chunks():
            o_ref[:, cols] = (acc * scale).astype(o_ref.dtype)

    @pl.when(j >= n_rope_tiles + n_plain_tiles)
    def _():
        for cols, acc in chunks():
            o_ref[:, cols] = jax.nn.sigmoid(acc).astype(o_ref.dtype)


def _proj(x2, posf, invf, g_mix, w_cat, *, tm, tn, n_rope_tiles, n_plain_tiles, sb_q_tile):
    t_tok, d = x2.shape
    n = w_cat.shape[1]
    kern = functools.partial(_proj_kernel, n_rope_tiles=n_rope_tiles, n_plain_tiles=n_plain_tiles,
                             sb_q_tile=sb_q_tile)
    return pl.pallas_call(
        kern,
        grid=(t_tok // tm, n // tn),
        in_specs=[
            pl.BlockSpec((tm, d), lambda i, j: (i, 0)),
            pl.BlockSpec((tm, 1), lambda i, j: (i, 0)),
            pl.BlockSpec((1, LANES), lambda i, j: (0, 0)),
            pl.BlockSpec((1, d), lambda i, j: (0, 0)),
            pl.BlockSpec((d, tn), lambda i, j: (0, j)),
        ],
        out_specs=pl.BlockSpec((tm, tn), lambda i, j: (i, j)),
        out_shape=jax.ShapeDtypeStruct((t_tok, n), BF16),
        scratch_shapes=[
            pltpu.VMEM((tm, d), BF16),
            pltpu.VMEM((tm, LANES), F32),
            pltpu.VMEM((tm, LANES), F32),
            pltpu.VMEM((tm, LANES), F32),
        ],
        compiler_params=_cparams(("parallel", "arbitrary")),
        name="proj",
    )(x2, posf, invf, g_mix, w_cat)


def _dattn_kernel(q_ref, k_ref, v_ref, lp_ref, g_ref, o_ref, s_ref, ml_ref, mb_ref, acc_ref, vx_ref, *, tq):
    s_len = q_ref.shape[1]
    n_heads = q_ref.shape[2] // LANES
    nq = s_len // tq
    nc = tq // LANES
    lp = lp_ref[...]
    lam = (jnp.exp(jnp.sum(lp[0:1] * lp[1:2], axis=-1, keepdims=True))
           - jnp.exp(jnp.sum(lp[2:3] * lp[3:4], axis=-1, keepdims=True)) + LAMBDA_INIT)
    lane = lax.broadcasted_iota(jnp.int32, (tq, LANES), 1)
    row = lax.broadcasted_iota(jnp.int32, (2 * tq, tq), 0) % tq
    col = lax.broadcasted_iota(jnp.int32, (2 * tq, tq), 1)
    causal = col <= row
    heads = [(hd, slice(hd * LANES, (hd + 1) * LANES)) for hd in range(n_heads)]
    for hd, cols in heads:
        vx_ref[hd, :, :DIFF_V_DIM] = v_ref[0, :, cols]
        vx_ref[hd, :, DIFF_V_DIM:] = jnp.ones((s_len, LANES), vx_ref.dtype)

    def rows(i):
        return pl.ds(pl.multiple_of(i * tq, tq), tq)

    def q_body(qi, carry):
        qqs = []
        for hd, cols in heads:
            q = q_ref[0, rows(qi), cols]
            zero = jnp.zeros_like(q)
            qqs.append(jnp.concatenate([jnp.where(lane < DIFF_HEAD_DIM, q, zero),
                                        jnp.where(lane >= DIFF_HEAD_DIM, q, zero)], axis=0))
        ml_ref[...] = jnp.full(ml_ref.shape, NEG_INF, F32)

        def scores(kj, masked):
            for hd, cols in heads:
                k = k_ref[0, rows(kj), cols]
                s = lax.dot_general(qqs[hd], k, (((1,), (1,)), ((), ())), preferred_element_type=F32)
                if masked:
                    s = jnp.where(causal, s, NEG_INF)
                s_ref[hd, kj] = s
                m = ml_ref[hd]
                for c in range(nc):
                    m = jnp.maximum(m, s[:, c * LANES:(c + 1) * LANES])
                ml_ref[hd] = m

        def s_body(kj, c):
            scores(kj, False)
            return c

        lax.fori_loop(0, qi, s_body, 0)
        scores(qi, True)

        for hd, _ in heads:
            mb_ref[hd] = jnp.broadcast_to(jnp.max(ml_ref[hd], axis=-1, keepdims=True), mb_ref.shape[1:])
        acc_ref[...] = jnp.zeros(acc_ref.shape, F32)

        def av_body(kj, c):
            for hd, _ in heads:
                mb = mb_ref[hd]
                p = jnp.concatenate(
                    [jnp.exp2(s_ref[hd, kj, :, c * LANES:(c + 1) * LANES] - mb) for c in range(nc)], axis=1)
                acc_ref[hd] += jnp.dot(p.astype(BF16), vx_ref[hd, rows(kj), :], preferred_element_type=F32)
            return c

        lax.fori_loop(0, qi + 1, av_body, 0)

        for hd, cols in heads:
            o = acc_ref[hd, :, :DIFF_V_DIM] / acc_ref[hd, :, DIFF_V_DIM:]
            od = o[:tq] - lam * o[tq:]
            od = _rms(od, g_ref[...]) * (1.0 - LAMBDA_INIT)
            o_ref[0, rows(qi), cols] = od.astype(o_ref.dtype)
        return carry

    lax.fori_loop(0, nq, q_body, 0)


def _dattn(proj3, lam_params, g_subln, *, tq, heads_per_step):
    b, s_len, _ = proj3.shape
    h = N_HEADS_DIFF
    hp = heads_per_step
    kern = functools.partial(_dattn_kernel, tq=tq)
    blk = lambda off: pl.BlockSpec((1, s_len, hp * LANES), lambda bi, hi: (bi, 0, off // hp + hi))
    return pl.pallas_call(
        kern,
        grid=(b, h // hp),
        in_specs=[
            blk(0), blk(h), blk(2 * h),
            pl.BlockSpec(lam_params.shape, lambda bi, hi: (0, 0)),
            pl.BlockSpec((1, DIFF_V_DIM), lambda bi, hi: (0, 0)),
        ],
        out_specs=pl.BlockSpec((1, s_len, hp * LANES), lambda bi, hi: (bi, 0, hi)),
        out_shape=jax.ShapeDtypeStruct((b, s_len, h * DIFF_V_DIM), BF16),
        scratch_shapes=[
            pltpu.VMEM((hp, s_len // tq, 2 * tq, tq), F32),
            pltpu.VMEM((hp, 2 * tq, LANES), F32),
            pltpu.VMEM((hp, 2 * tq, LANES), F32),
            pltpu.VMEM((hp, 2 * tq, DIFF_V_DIM + LANES), F32),
            pltpu.VMEM((hp, s_len, DIFF_V_DIM + LANES), BF16),
        ],
        compiler_params=_cparams(("parallel", "parallel")),
        name="dattn",
    )(proj3, proj3, proj3, lam_params, g_subln)


def _sb_terms(q, k, u, strict):
    z = lax.dot_general(q, k, (((1,), (1,)), ((), ())), preferred_element_type=F32)
    l1p = jnp.log2(1.0 + jnp.exp2(-jnp.abs(z)))
    lom = jnp.minimum(-z, 0.0) - l1p
    log_beta = lom + z
    if strict is not None:
        lom = jnp.where(strict, lom, 0.0)
    hi = lom.astype(BF16)
    lo = (lom - hi.astype(F32)).astype(BF16)
    suffix = jnp.dot(hi, u, preferred_element_type=F32) + jnp.dot(lo, u, preferred_element_type=F32)
    return log_beta, lom, suffix


def _lane_rowsum(x):
    return jnp.broadcast_to(jnp.sum(x, axis=-1, keepdims=True), (x.shape[0], LANES))


def _exp_shifted(x, rb):
    nc = x.shape[1] // LANES
    return jnp.concatenate([jnp.exp2(x[:, c * LANES:(c + 1) * LANES] + rb) for c in range(nc)], axis=1)


def _sbattn_kernel(q_ref, k_ref, v_ref, u_ref, o_ref, rb_ref, acc_ref, *, tq):
    s_len = q_ref.shape[1]
    n_heads = q_ref.shape[2] // LANES
    nq = s_len // tq
    row = lax.broadcasted_iota(jnp.int32, (tq, tq), 0)
    col = lax.broadcasted_iota(jnp.int32, (tq, tq), 1)
    strict = col < row
    heads = [(hd, slice(hd * LANES, (hd + 1) * LANES)) for hd in range(n_heads)]

    def rows(i):
        return pl.ds(pl.multiple_of(i * tq, tq), tq)

    def q_tile(qi, first):
        u = u_ref[...]
        rmax = None
        for hd, cols in heads:
            q = q_ref[0, rows(qi), cols]
            log_beta, lom, suffix = _sb_terms(q, k_ref[0, rows(qi), cols], u, strict)
            a = jnp.where(strict, jnp.exp2(log_beta + suffix), 0.0)
            acc = jnp.dot(a.astype(BF16), v_ref[0, rows(qi), cols], preferred_element_type=F32)
            rb = _lane_rowsum(lom)
            if first:
                o_ref[0, rows(qi), cols] = acc.astype(o_ref.dtype)
                continue
            log_beta, lom, suffix = _sb_terms(q, k_ref[0, rows(qi - 1), cols], u, None)
            a = _exp_shifted(log_beta + suffix, rb)
            acc_ref[hd] = acc + jnp.dot(a.astype(BF16), v_ref[0, rows(qi - 1), cols],
                                        preferred_element_type=F32)
            rb = rb + _lane_rowsum(lom)
            rb_ref[hd] = rb
            rmax = jnp.max(rb) if rmax is None else jnp.maximum(rmax, jnp.max(rb))
        if first:
            return

        def cond(c):
            n, rmax = c
            return (n <= qi) & (rmax > SB_CUTOFF)

        def body(c):
            n, _ = c
            kj = qi - n
            rmax = None
            for hd, cols in heads:
                log_beta, lom, suffix = _sb_terms(q_ref[0, rows(qi), cols], k_ref[0, rows(kj), cols],
                                                  u_ref[...], None)
                a = _exp_shifted(log_beta + suffix, rb_ref[hd])
                acc_ref[hd] += jnp.dot(a.astype(BF16), v_ref[0, rows(kj), cols], preferred_element_type=F32)
                rb_new = rb_ref[hd] + _lane_rowsum(lom)
                rb_ref[hd] = rb_new
                rmax = jnp.max(rb_new) if rmax is None else jnp.maximum(rmax, jnp.max(rb_new))
            return n + 1, rmax

        lax.while_loop(cond, body, (jnp.int32(2), rmax))
        for hd, cols in heads:
            o_ref[0, rows(qi), cols] = acc_ref[hd].astype(o_ref.dtype)

    q_tile(0, True)

    def q_body(qi, carry):
        q_tile(qi, False)
        return carry

    lax.fori_loop(1, nq, q_body, 0)


def _sbattn(proj3, tri, *, tq, col0, heads_per_step):
    b, s_len, _ = proj3.shape
    h = N_HEADS_SB
    hp = heads_per_step
    kern = functools.partial(_sbattn_kernel, tq=tq)
    blk = lambda off: pl.BlockSpec((1, s_len, hp * LANES), lambda bi, hi: (bi, 0, off // hp + hi))
    return pl.pallas_call(
        kern,
        grid=(b, h // hp),
        in_specs=[
            blk(col0), blk(col0 + h), blk(col0 + 2 * h),
            pl.BlockSpec((tq, tq), lambda bi, hi: (0, 0)),
        ],
        out_specs=pl.BlockSpec((1, s_len, hp * LANES), lambda bi, hi: (bi, 0, hi)),
        out_shape=jax.ShapeDtypeStruct((b, s_len, h * SB_HEAD_DIM), BF16),
        scratch_shapes=[
            pltpu.VMEM((hp, tq, LANES), F32),
            pltpu.VMEM((hp, tq, SB_HEAD_DIM), F32),
        ],
        compiler_params=_cparams(("parallel", "parallel")),
        name="sbattn",
    )(proj3, proj3, proj3, tri)


def _split_bf16(a):
    hi = a.astype(BF16)
    return hi, (a - hi.astype(F32)).astype(BF16)


def _mix_kernel(od_ref, os_ref, g0_ref, g1_ref, x_ref, wd_ref, ws_ref, wo_ref, gm_ref,
                wr_hi_ref, wr_lo_ref, br_ref, ltri_ref, x1_ref, hn_ref, route_ref, cnt_ref):
    @pl.when(pl.program_id(0) == 0)
    def _():
        cnt_ref[...] = jnp.zeros(cnt_ref.shape, F32)

    yd = jnp.dot(od_ref[...], wd_ref[...], preferred_element_type=F32)
    ys = jnp.dot(os_ref[...], ws_ref[...], preferred_element_type=F32)
    mixed = g0_ref[...].astype(F32) * yd + g1_ref[...].astype(F32) * ys
    x1 = x_ref[...] + jnp.dot(mixed.astype(BF16), wo_ref[...], preferred_element_type=F32)
    x1_ref[...] = x1
    hn = _rms(x1, gm_ref[...])
    hn_hi, hn_lo = _split_bf16(hn)
    tm = hn.shape[0]
    hn_f = hn_hi.astype(F32)
    for c in range(SLAB_PITCH):
        part = hn_f[:, c * LANES:(c + 1) * LANES] if c < SLAB_ROWS else jnp.zeros((tm, LANES), F32)
        hn_ref[pl.ds(c, tm, stride=SLAB_PITCH), :] = part

    quad = jnp.dot(jnp.concatenate([hn_hi, hn_lo], axis=0),
                   jnp.concatenate([wr_hi_ref[...], wr_lo_ref[...]], axis=1), preferred_element_type=F32)
    logits = (quad[:tm, :LANES] + quad[:tm, LANES:]) + (quad[tm:, :LANES] + quad[tm:, LANES:]) + br_ref[...]
    lane = lax.broadcasted_iota(jnp.int32, logits.shape, 1)
    big = jnp.int32(LANES)

    def first_argmax(vals, vmax):
        return jnp.min(jnp.where(vals == vmax, lane, big), axis=-1, keepdims=True)

    is_group = (lane >= N_EXPERTS) & (lane < N_EXPERTS + N_GROUPS)
    gl = jnp.where(is_group, logits, -jnp.inf)
    gmax = jnp.max(gl, axis=-1, keepdims=True)
    gexp = jnp.exp(gl - gmax)
    probs = gexp / jnp.sum(gexp, axis=-1, keepdims=True)
    g_val = jnp.max(probs, axis=-1, keepdims=True)
    g_idx = first_argmax(jnp.where(is_group, probs, -jnp.inf), g_val) - N_EXPERTS

    in_group = (lane >= g_idx * EXPERTS_PER_GROUP) & (lane < (g_idx + 1) * EXPERTS_PER_GROUP)
    el = jnp.where(in_group, logits, -jnp.inf)
    v1 = jnp.max(el, axis=-1, keepdims=True)
    i1 = first_argmax(el, v1)
    el2 = jnp.where(lane == i1, -jnp.inf, el)
    v2 = jnp.max(el2, axis=-1, keepdims=True)
    i2 = first_argmax(el2, v2)
    e2 = jnp.exp(v2 - v1)
    den = 1.0 + e2
    w1 = (1.0 / den) * g_val
    w2 = (e2 / den) * g_val

    pick1 = lane == i1
    pick2 = lane == i2
    onehot = jnp.where(pick1 | pick2, 1.0, 0.0).astype(BF16)
    seen = jnp.dot(ltri_ref[...], onehot, preferred_element_type=F32) + cnt_ref[...]
    rank1 = jnp.sum(jnp.where(pick1, seen, 0.0), axis=-1, keepdims=True) - 1.0
    rank2 = jnp.sum(jnp.where(pick2, seen, 0.0), axis=-1, keepdims=True) - 1.0
    cnt_ref[...] = seen[seen.shape[0] - 1:, :]
    cols = (w1, w2, i1.astype(F32), i2.astype(F32), rank1, rank2)
    route = jnp.zeros(logits.shape, F32)
    for c, val in enumerate(cols):
        route = jnp.where(lane == c, val, route)
    route_ref[...] = route


def _mix(o_diff, o_sb, proj, x2, w_brd, w_brs, w_out, g_moe, wr_hi, wr_lo, b_r, *, tm, gate_col0):
    t_tok, d = x2.shape
    wd_in = o_diff.shape[1]
    ws_in = o_sb.shape[1]
    gblk = gate_col0 // d
    r = np.arange(tm)
    ltri = jnp.asarray(r[None, :] <= r[:, None], dtype=BF16)
    const = lambda shape: pl.BlockSpec(shape, lambda i: (0, 0), pipeline_mode=pl.Buffered(1))
    return pl.pallas_call(
        _mix_kernel,
        grid=(t_tok // tm,),
        in_specs=[
            pl.BlockSpec((tm, wd_in), lambda i: (i, 0)),
            pl.BlockSpec((tm, ws_in), lambda i: (i, 0)),
            pl.BlockSpec((tm, d), lambda i: (i, gblk)),
            pl.BlockSpec((tm, d), lambda i: (i, gblk + 1)),
            pl.BlockSpec((tm, d), lambda i: (i, 0)),
            const(w_brd.shape), const(w_brs.shape), const(w_out.shape),
            const(g_moe.shape), const(wr_hi.shape), const(wr_lo.shape), const(b_r.shape),
            const(ltri.shape),
        ],
        out_specs=[
            pl.BlockSpec((tm, d), lambda i: (i, 0)),
            pl.BlockSpec((tm * SLAB_PITCH, LANES), lambda i: (i, 0)),
            pl.BlockSpec((tm, LANES), lambda i: (i, 0)),
            pl.BlockSpec((1, LANES), lambda i: (0, 0)),
        ],
        out_shape=[
            jax.ShapeDtypeStruct((t_tok, d), F32),
            jax.ShapeDtypeStruct((t_tok * SLAB_PITCH, LANES), F32),
            jax.ShapeDtypeStruct((t_tok, LANES), F32),
            jax.ShapeDtypeStruct((1, LANES), F32),
        ],
        compiler_params=_cparams(("arbitrary",)),
        name="mix",
    )(o_diff, o_sb, proj, proj, x2, w_brd, w_brs, w_out, g_moe, wr_hi, wr_lo, b_r, ltri)


def _gather_rows(src_hbm, idx_ref, base, dst, sem, n_rows, first_row=0):
    for r in range(first_row, n_rows):
        row = idx_ref[base + r]
        pltpu.make_async_copy(src_hbm.at[pl.ds(row, 1), :], dst.at[pl.ds(r, 1), :], sem).start()


def _wait_rows(src_hbm, dst, sem):
    pltpu.make_async_copy(src_hbm.at[pl.ds(0, dst.shape[0]), :], dst, sem).wait()


def _gather_slabs(src_hbm, idx_ref, base, dst, sem, n_items):
    for r in range(n_items):
        tok = idx_ref[base + r]
        pltpu.make_async_copy(src_hbm.at[pl.ds(tok * SLAB_PITCH, SLAB_ROWS), :],
                              dst.at[pl.ds(r * SLAB_PITCH, SLAB_ROWS), :], sem).start()


def _wait_slabs(src_hbm, dst, sem, n_items):
    n = n_items * SLAB_ROWS
    pltpu.make_async_copy(src_hbm.at[pl.ds(0, n), :], dst.at[pl.ds(0, n), :], sem).wait()


def _experts_kernel(te_ref, nused_ref, src_ref, hn_hbm, wg_ref, wu_ref, wd_ref, y_ref, xbuf, sem):
    i = pl.program_id(0)
    tm = y_ref.shape[0]
    n_used = nused_ref[0]
    slot = i % 2

    @pl.when(i == 0)
    def _():
        _gather_slabs(hn_hbm, src_ref, 0, xbuf.at[0], sem.at[0], tm)

    @pl.when(i + 1 < n_used)
    def _():
        _gather_slabs(hn_hbm, src_ref, (i + 1) * tm, xbuf.at[1 - slot], sem.at[1 - slot], tm)

    @pl.when(i < n_used)
    def _():
        _wait_slabs(hn_hbm, xbuf.at[slot], sem.at[slot], tm)
        x = jnp.concatenate([xbuf[slot, pl.ds(c, tm, stride=SLAB_PITCH), :] for c in range(SLAB_ROWS)],
                            axis=1).astype(BF16)
        g = jnp.dot(x, wg_ref[0], preferred_element_type=F32)
        u = jnp.dot(x, wu_ref[0], preferred_element_type=F32)
        a = (jax.nn.silu(g) * u).astype(BF16)
        y_ref[...] = jnp.dot(a, wd_ref[0], preferred_element_type=F32)

    @pl.when(i >= n_used)
    def _():
        y_ref[...] = jnp.zeros(y_ref.shape, y_ref.dtype)


def _experts(tile_expert, n_used, src, hn_slabs, w_g, w_u, w_d, *, tm):
    n_rows = src.shape[0]
    n_e, d, ff = w_g.shape
    assert d == SLAB_ROWS * LANES
    grid_spec = pltpu.PrefetchScalarGridSpec(
        num_scalar_prefetch=3,
        grid=(n_rows // tm,),
        in_specs=[
            pl.BlockSpec(memory_space=pl.ANY),
            pl.BlockSpec((1, d, ff), lambda i, te, nu, sr: (te[i], 0, 0)),
            pl.BlockSpec((1, d, ff), lambda i, te, nu, sr: (te[i], 0, 0)),
            pl.BlockSpec((1, ff, d), lambda i, te, nu, sr: (te[i], 0, 0)),
        ],
        out_specs=pl.BlockSpec((tm, d), lambda i, te, nu, sr: (i, 0)),
        scratch_shapes=[pltpu.VMEM((2, tm * SLAB_PITCH, LANES), F32), pltpu.SemaphoreType.DMA((2,))],
    )
    return pl.pallas_call(
        _experts_kernel,
        grid_spec=grid_spec,
        out_shape=jax.ShapeDtypeStruct((n_rows, d), F32),
        compiler_params=pltpu.CompilerParams(dimension_semantics=("arbitrary",),
                                             vmem_limit_bytes=VMEM_LIMIT, disable_bounds_checks=True),
        name="experts",
    )(tile_expert, n_used, src, hn_slabs, w_g, w_u, w_d)


def _combine_kernel(d1_ref, d2_ref, y_hbm, x1_ref, route_ref, gf_ref, o_ref, ybuf, sem):
    i = pl.program_id(0)
    n = pl.num_programs(0)
    tm = x1_ref.shape[0]
    slot = i % 2

    def gather(tile, s):
        _gather_rows(y_hbm, d1_ref, tile * tm, ybuf.at[s, 0], sem.at[s], tm)
        _gather_rows(y_hbm, d2_ref, tile * tm, ybuf.at[s, 1], sem.at[s], tm)

    @pl.when(i == 0)
    def _():
        gather(0, 0)

    @pl.when(i + 1 < n)
    def _():
        gather(i + 1, 1 - slot)

    _wait_rows(y_hbm, ybuf.at[slot, 0], sem.at[slot])
    _wait_rows(y_hbm, ybuf.at[slot, 1], sem.at[slot])
    route = route_ref[...]
    x2 = x1_ref[...] + route[:, 0:1] * ybuf[slot, 0] + route[:, 1:2] * ybuf[slot, 1]
    o_ref[...] = _rms(x2, gf_ref[...])


def _combine(d1, d2, y, x1, route, g_final, *, tm):
    t_tok, d = x1.shape
    grid_spec = pltpu.PrefetchScalarGridSpec(
        num_scalar_prefetch=2,
        grid=(t_tok // tm,),
        in_specs=[
            pl.BlockSpec(memory_space=pl.ANY),
            pl.BlockSpec((tm, d), lambda i, a, b: (i, 0)),
            pl.BlockSpec((tm, LANES), lambda i, a, b: (i, 0)),
            pl.BlockSpec((1, d), lambda i, a, b: (0, 0)),
        ],
        out_specs=pl.BlockSpec((tm, d), lambda i, a, b: (i, 0)),
        scratch_shapes=[pltpu.VMEM((2, 2, tm, d), F32), pltpu.SemaphoreType.DMA((2,))],
    )
    return pl.pallas_call(
        _combine_kernel,
        grid_spec=grid_spec,
        out_shape=jax.ShapeDtypeStruct((t_tok, d), F32),
        compiler_params=pltpu.CompilerParams(dimension_semantics=("arbitrary",),
                                             vmem_limit_bytes=VMEM_LIMIT, disable_bounds_checks=True),
        name="combine",
    )(d1, d2, y, x1, route, g_final)


def _pair_rows_kernel(route_ref, off_ref, d_ref):
    route = route_ref[...]
    off = off_ref[...]
    lane = lax.broadcasted_iota(jnp.int32, route.shape, 1)
    lane_f = lane.astype(F32)

    def row_of(e_col, rank_col):
        first = jnp.sum(jnp.where(lane_f == route[:, e_col:e_col + 1], off, 0.0), axis=-1, keepdims=True)
        return first + route[:, rank_col:rank_col + 1]

    d = jnp.where(lane == 0, row_of(2, 4), jnp.where(lane == 1, row_of(3, 5), 0.0))
    d_ref[...] = d.astype(jnp.int32)


def _pair_rows(route, off_row, *, tm):
    t_tok = route.shape[0]
    return pl.pallas_call(
        _pair_rows_kernel,
        grid=(t_tok // tm,),
        in_specs=[pl.BlockSpec((tm, LANES), lambda i: (i, 0)), pl.BlockSpec((1, LANES), lambda i: (0, 0))],
        out_specs=pl.BlockSpec((tm, LANES), lambda i: (i, 0)),
        out_shape=jax.ShapeDtypeStruct((t_tok, LANES), jnp.int32),
        compiler_params=_cparams(("parallel",)),
        name="pair_rows",
    )(route, off_row)


def _dispatch_plan(route, counts, *, tm, n_tiles):
    t_tok = route.shape[0]
    cnt = counts[0, :N_EXPERTS].astype(jnp.int32)
    tiles_per_e = (cnt + tm - 1) // tm
    tile_end = jnp.cumsum(tiles_per_e)
    off = (tile_end - tiles_per_e) * tm
    off_row = jnp.pad(off.astype(F32), (0, LANES - N_EXPERTS)).reshape(1, LANES)
    pair_rows = _pair_rows(route, off_row, tm=_tile(t_tok, 2048))
    d1 = pair_rows[:, 0]
    d2 = pair_rows[:, 1]
    tok = jnp.arange(t_tok, dtype=jnp.int32)
    src = jnp.zeros((n_tiles * tm,), jnp.int32).at[jnp.concatenate([d1, d2])].add(
        jnp.concatenate([tok, tok]))
    tile_ids = jnp.arange(n_tiles, dtype=jnp.int32)
    tile_expert = jnp.minimum(
        jnp.sum((tile_end[None, :] <= tile_ids[:, None]).astype(jnp.int32), axis=1), N_EXPERTS - 1)
    return d1, d2, src, tile_expert, tile_end[-1:].astype(jnp.int32)


def _tile(n, pref):
    t = min(n, pref)
    assert n % t == 0, (n, t)
    return t


def kernel(x, positions, g_mix, w_in, w_gate, lambda_params, g_subln, w_br_diff, w_br_sb, w_out,
           g_moe, w_group, b_group, w_router, b_router, w_up_gate, w_up, w_down, g_final):
    b, s_len, d = x.shape
    t_tok = b * s_len
    depth = w_in.shape[0]
    assert depth == 1
    l = 0
    x2 = x.reshape(t_tok, d)

    inv_freq = ROPE_THETA ** (-jnp.arange(0, ROT_DIM, 2, dtype=F32) / ROT_DIM)
    lane = np.arange(LANES)
    invf = inv_freq[(lane % DIFF_HEAD_DIM) % (ROT_DIM // 2)].reshape(1, LANES)
    posf = positions.astype(F32).reshape(t_tok, 1)

    in_w = w_in.shape[2]
    w_cat = jnp.concatenate([w_in[l].astype(BF16), w_gate[l].astype(BF16)], axis=1)
    tn = 1024
    qk_w = N_HEADS_DIFF * 2 * DIFF_HEAD_DIM
    proj = _proj(x2, posf, invf, g_mix[l].reshape(1, d), w_cat,
                 tm=_tile(t_tok, 512), tn=tn,
                 n_rope_tiles=2 * qk_w // tn, n_plain_tiles=(in_w - 2 * qk_w) // tn,
                 sb_q_tile=(2 * qk_w + N_HEADS_DIFF * DIFF_V_DIM) // tn)
    proj3 = proj.reshape(b, s_len, proj.shape[1])

    o_diff = _dattn(proj3, lambda_params[l], g_subln[l].reshape(1, DIFF_V_DIM), tq=_tile(s_len, 512),
                    heads_per_step=2)
    tq = _tile(s_len, 256)
    r = np.arange(tq)
    tri = jnp.asarray(r[:, None] > r[None, :], dtype=BF16)
    sb_col0 = (2 * qk_w + N_HEADS_DIFF * DIFF_V_DIM) // LANES
    o_sb = _sbattn(proj3, tri, tq=tq, col0=sb_col0, heads_per_step=4)

    w_r = jnp.concatenate([w_router[l], w_group[l]], axis=1)
    w_r = jnp.pad(w_r, ((0, 0), (0, LANES - w_r.shape[1])))
    wr_hi = w_r.astype(BF16)
    wr_lo = (w_r - wr_hi.astype(F32)).astype(BF16)
    b_r = jnp.pad(jnp.concatenate([b_router[l], b_group[l]]), (0, LANES - N_EXPERTS - N_GROUPS))
    x1, hn, route, counts = _mix(
        o_diff.reshape(t_tok, -1), o_sb.reshape(t_tok, -1), proj, x2,
        w_br_diff[l].astype(BF16), w_br_sb[l].astype(BF16), w_out[l].astype(BF16),
        g_moe[l].reshape(1, d), wr_hi, wr_lo, b_r.reshape(1, LANES),
        tm=_tile(t_tok, 256), gate_col0=in_w)

    tm_e = _tile(t_tok, 512)
    n_tiles = 2 * t_tok // tm_e + N_EXPERTS
    d1, d2, src, tile_expert, n_used = _dispatch_plan(route, counts, tm=tm_e, n_tiles=n_tiles)
    y = _experts(tile_expert, n_used, src, hn,
                 w_up_gate[l].astype(BF16), w_up[l].astype(BF16), w_down[l].astype(BF16), tm=tm_e)
    out = _combine(d1, d2, y, x1, route, g_final.reshape(1, d), tm=_tile(t_tok, 256))
    return out.reshape(b, s_len, d)
```

```python
import functools

import numpy as np
import jax
import jax.numpy as jnp
from jax import lax
from jax.experimental import pallas as pl
from jax.experimental.pallas import tpu as pltpu

F32 = jnp.float32
BF16 = jnp.bfloat16

N_HEADS_DIFF = 8
DIFF_HEAD_DIM = 64
DIFF_V_DIM = 128
N_HEADS_SB = 8
SB_HEAD_DIM = 128
ROT_DIM = 16
ROPE_THETA = 500000.0
N_GROUPS = 4
EXPERTS_PER_GROUP = 8
N_EXPERTS = 32
EPS = 1e-6
NEG_INF = float(np.finfo(np.float32).min)
SB_CUTOFF = -150.5
LAMBDA_INIT = 0.8 - 0.6 * float(np.exp(-0.3 * 0))
LOG2E = float(np.log2(np.e))

LANES = 128
MXU_N = 256
SLAB_ROWS = 16
SLAB_PITCH = 20
VMEM_LIMIT = 56 * 1024 * 1024


def _cparams(sem):
    return pltpu.CompilerParams(dimension_semantics=sem, vmem_limit_bytes=VMEM_LIMIT)


def _rms(x, g):
    return x * lax.rsqrt(jnp.mean(x * x, axis=-1, keepdims=True) + EPS) * g


def _proj_kernel(x_ref, pos_ref, invf_ref, g_ref, w_ref, o_ref, h_ref, c_ref, s1_ref, s2_ref,
                 *, n_rope_tiles, n_plain_tiles, dq_cols, sbq_cols):
    j = pl.program_id(1)
    tn = o_ref.shape[1]

    @pl.when(j == 0)
    def _():
        h_ref[...] = _rms(x_ref[...], g_ref[...]).astype(BF16)
        ang = pos_ref[...] * invf_ref[...]
        lane = lax.broadcasted_iota(jnp.int32, ang.shape, 1) % DIFF_HEAD_DIM
        cos = jnp.cos(ang)
        sin = jnp.sin(ang)
        half = ROT_DIM // 2
        c_ref[...] = jnp.where(lane < ROT_DIM, cos, 1.0)
        s1_ref[...] = jnp.where(lane < half, -sin, 0.0)
        s2_ref[...] = jnp.where((lane >= half) & (lane < ROT_DIM), sin, 0.0)

    def chunks():
        h = h_ref[...]
        for c in range(tn // MXU_N):
            cols = slice(c * MXU_N, (c + 1) * MXU_N)
            yield cols, j * tn + c * MXU_N, jnp.dot(h, w_ref[:, cols], preferred_element_type=F32)

    def scale_in(col0, col_range, value):
        return jnp.where((col0 >= col_range[0]) & (col0 < col_range[1]), value, 1.0).astype(F32)

    @pl.when(j < n_rope_tiles)
    def _():
        half = ROT_DIM // 2
        for cols, col0, acc in chunks():
            qscale = scale_in(col0, dq_cols, DIFF_HEAD_DIM ** -0.5 * LOG2E)
            parts = []
            for c in range(MXU_N // LANES):
                t = acc[:, c * LANES:(c + 1) * LANES]
                up = pltpu.roll(t, LANES - half, axis=1)
                dn = pltpu.roll(t, half, axis=1)
                parts.append((t * c_ref[...] + up * s1_ref[...] + dn * s2_ref[...]) * qscale)
            o_ref[:, cols] = jnp.concatenate(parts, axis=1).astype(o_ref.dtype)

    @pl.when((j >= n_rope_tiles) & (j < n_rope_tiles + n_plain_tiles))
    def _():
        for cols, col0, acc in chunks():
            scale = scale_in(col0, sbq_cols, SB_HEAD_DIM ** -0.5 * LOG2E)
            o_ref[:, cols] = (acc * scale).astype(o_ref.dtype)

    @pl.when(j >= n_rope_tiles + n_plain_tiles)
    def _():
        for cols, _, acc in chunks():
            o_ref[:, cols] = jax.nn.sigmoid(acc).astype(o_ref.dtype)


def _proj(x2, posf, invf, g_mix, w_cat, *, tm, tn, n_rope_tiles, n_plain_tiles, dq_cols, sbq_cols):
    t_tok, d = x2.shape
    n = w_cat.shape[1]
    kern = functools.partial(_proj_kernel, n_rope_tiles=n_rope_tiles, n_plain_tiles=n_plain_tiles,
                             dq_cols=dq_cols, sbq_cols=sbq_cols)
    return pl.pallas_call(
        kern,
        grid=(t_tok // tm, n // tn),
        in_specs=[
            pl.BlockSpec((tm, d), lambda i, j: (i, 0)),
            pl.BlockSpec((tm, 1), lambda i, j: (i, 0)),
            pl.BlockSpec((1, LANES), lambda i, j: (0, 0)),
            pl.BlockSpec((1, d), lambda i, j: (0, 0)),
            pl.BlockSpec((d, tn), lambda i, j: (0, j)),
        ],
        out_specs=pl.BlockSpec((tm, tn), lambda i, j: (i, j)),
        out_shape=jax.ShapeDtypeStruct((t_tok, n), BF16),
        scratch_shapes=[
            pltpu.VMEM((tm, d), BF16),
            pltpu.VMEM((tm, LANES), F32),
            pltpu.VMEM((tm, LANES), F32),
            pltpu.VMEM((tm, LANES), F32),
        ],
        compiler_params=_cparams(("parallel", "arbitrary")),
        name="proj",
    )(x2, posf, invf, g_mix, w_cat)


def _dattn_kernel(q_ref, k_ref, v_ref, lp_ref, g_ref, o_ref, s_ref, ml_ref, mb_ref, acc_ref, vx_ref, *, tq):
    s_len = q_ref.shape[1]
    n_heads = q_ref.shape[2] // LANES
    nq = s_len // tq
    nc = tq // LANES
    lp = lp_ref[...]
    lam = (jnp.exp(jnp.sum(lp[0:1] * lp[1:2], axis=-1, keepdims=True))
           - jnp.exp(jnp.sum(lp[2:3] * lp[3:4], axis=-1, keepdims=True)) + LAMBDA_INIT)
    lane = lax.broadcasted_iota(jnp.int32, (tq, LANES), 1)
    row = lax.broadcasted_iota(jnp.int32, (2 * tq, tq), 0) % tq
    col = lax.broadcasted_iota(jnp.int32, (2 * tq, tq), 1)
    causal = col <= row
    heads = [(hd, slice(hd * LANES, (hd + 1) * LANES)) for hd in range(n_heads)]
    for hd, cols in heads:
        vx_ref[hd, :, :DIFF_V_DIM] = v_ref[0, :, cols]
        vx_ref[hd, :, DIFF_V_DIM:] = jnp.ones((s_len, LANES), vx_ref.dtype)

    def rows(i):
        return pl.ds(pl.multiple_of(i * tq, tq), tq)

    def q_body(qi, carry):
        qqs = []
        for hd, cols in heads:
            q = q_ref[0, rows(qi), cols]
            zero = jnp.zeros_like(q)
            qqs.append(jnp.concatenate([jnp.where(lane < DIFF_HEAD_DIM, q, zero),
                                        jnp.where(lane >= DIFF_HEAD_DIM, q, zero)], axis=0))
        ml_ref[...] = jnp.full(ml_ref.shape, NEG_INF, F32)

        def scores(kj, masked):
            for hd, cols in heads:
                k = k_ref[0, rows(kj), cols]
                s = lax.dot_general(qqs[hd], k, (((1,), (1,)), ((), ())), preferred_element_type=F32)
                if masked:
                    s = jnp.where(causal, s, NEG_INF)
                s_ref[hd, kj] = s
                m = ml_ref[hd]
                for c in range(nc):
                    m = jnp.maximum(m, s[:, c * LANES:(c + 1) * LANES])
                ml_ref[hd] = m

        def s_body(kj, c):
            scores(kj, False)
            return c

        lax.fori_loop(0, qi, s_body, 0)
        scores(qi, True)

        for hd, _ in heads:
            mb_ref[hd] = jnp.broadcast_to(jnp.max(ml_ref[hd], axis=-1, keepdims=True), mb_ref.shape[1:])
        acc_ref[...] = jnp.zeros(acc_ref.shape, F32)

        def av_body(kj, c):
            for hd, _ in heads:
                mb = mb_ref[hd]
                p = jnp.concatenate(
                    [jnp.exp2(s_ref[hd, kj, :, c * LANES:(c + 1) * LANES] - mb) for c in range(nc)], axis=1)
                acc_ref[hd] += jnp.dot(p.astype(BF16), vx_ref[hd, rows(kj), :], preferred_element_type=F32)
            return c

        lax.fori_loop(0, qi + 1, av_body, 0)

        for hd, cols in heads:
            o = acc_ref[hd, :, :DIFF_V_DIM] / acc_ref[hd, :, DIFF_V_DIM:]
            od = o[:tq] - lam * o[tq:]
            od = _rms(od, g_ref[...]) * (1.0 - LAMBDA_INIT)
            o_ref[0, rows(qi), cols] = od.astype(o_ref.dtype)
        return carry

    lax.fori_loop(0, nq, q_body, 0)


def _dattn(proj3, lam_params, g_subln, *, tq, heads_per_step):
    b, s_len, _ = proj3.shape
    h = N_HEADS_DIFF
    hp = heads_per_step
    kern = functools.partial(_dattn_kernel, tq=tq)
    blk = lambda off: pl.BlockSpec((1, s_len, hp * LANES), lambda bi, hi: (bi, 0, off // hp + hi))
    return pl.pallas_call(
        kern,
        grid=(b, h // hp),
        in_specs=[
            blk(0), blk(h), blk(2 * h),
            pl.BlockSpec(lam_params.shape, lambda bi, hi: (0, 0)),
            pl.BlockSpec((1, DIFF_V_DIM), lambda bi, hi: (0, 0)),
        ],
        out_specs=pl.BlockSpec((1, s_len, hp * LANES), lambda bi, hi: (bi, 0, hi)),
        out_shape=jax.ShapeDtypeStruct((b, s_len, h * DIFF_V_DIM), BF16),
        scratch_shapes=[
            pltpu.VMEM((hp, s_len // tq, 2 * tq, tq), F32),
            pltpu.VMEM((hp, 2 * tq, LANES), F32),
            pltpu.VMEM((hp, 2 * tq, LANES), F32),
            pltpu.VMEM((hp, 2 * tq, DIFF_V_DIM + LANES), F32),
            pltpu.VMEM((hp, s_len, DIFF_V_DIM + LANES), BF16),
        ],
        compiler_params=_cparams(("parallel", "parallel")),
        name="dattn",
    )(proj3, proj3, proj3, lam_params, g_subln)


def _sb_terms(q, k, u, strict):
    z = lax.dot_general(q, k, (((1,), (1,)), ((), ())), preferred_element_type=F32)
    l1p = jnp.log2(1.0 + jnp.exp2(-jnp.abs(z)))
    lom = jnp.minimum(-z, 0.0) - l1p
    log_beta = lom + z
    if strict is not None:
        lom = jnp.where(strict, lom, 0.0)
    hi = lom.astype(BF16)
    lo = (lom - hi.astype(F32)).astype(BF16)
    suffix = jnp.dot(hi, u, preferred_element_type=F32) + jnp.dot(lo, u, preferred_element_type=F32)
    return log_beta, lom, suffix


def _lane_rowsum(x):
    return jnp.broadcast_to(jnp.sum(x, axis=-1, keepdims=True), (x.shape[0], LANES))


def _exp_shifted(x, rb):
    nc = x.shape[1] // LANES
    return jnp.concatenate([jnp.exp2(x[:, c * LANES:(c + 1) * LANES] + rb) for c in range(nc)], axis=1)


def _sbattn_kernel(q_ref, k_ref, v_ref, u_ref, o_ref, rb_ref, acc_ref, *, tq):
    s_len = q_ref.shape[1]
    n_heads = q_ref.shape[2] // LANES
    nq = s_len // tq
    row = lax.broadcasted_iota(jnp.int32, (tq, tq), 0)
    col = lax.broadcasted_iota(jnp.int32, (tq, tq), 1)
    strict = col < row
    heads = [(hd, slice(hd * LANES, (hd + 1) * LANES)) for hd in range(n_heads)]

    def rows(i):
        return pl.ds(pl.multiple_of(i * tq, tq), tq)

    def q_tile(qi, first):
        u = u_ref[...]
        rmax = None
        for hd, cols in heads:
            q = q_ref[0, rows(qi), cols]
            log_beta, lom, suffix = _sb_terms(q, k_ref[0, rows(qi), cols], u, strict)
            a = jnp.where(strict, jnp.exp2(log_beta + suffix), 0.0)
            acc = jnp.dot(a.astype(BF16), v_ref[0, rows(qi), cols], preferred_element_type=F32)
            rb = _lane_rowsum(lom)
            if first:
                o_ref[0, rows(qi), cols] = acc.astype(o_ref.dtype)
                continue
            log_beta, lom, suffix = _sb_terms(q, k_ref[0, rows(qi - 1), cols], u, None)
            a = _exp_shifted(log_beta + suffix, rb)
            acc_ref[hd] = acc + jnp.dot(a.astype(BF16), v_ref[0, rows(qi - 1), cols],
                                        preferred_element_type=F32)
            rb = rb + _lane_rowsum(lom)
            rb_ref[hd] = rb
            rmax = jnp.max(rb) if rmax is None else jnp.maximum(rmax, jnp.max(rb))
        if first:
            return

        def cond(c):
            n, rmax = c
            return (n <= qi) & (rmax > SB_CUTOFF)

        def body(c):
            n, _ = c
            kj = qi - n
            rmax = None
            for hd, cols in heads:
                log_beta, lom, suffix = _sb_terms(q_ref[0, rows(qi), cols], k_ref[0, rows(kj), cols],
                                                  u_ref[...], None)
                a = _exp_shifted(log_beta + suffix, rb_ref[hd])
                acc_ref[hd] += jnp.dot(a.astype(BF16), v_ref[0, rows(kj), cols], preferred_element_type=F32)
                rb_new = rb_ref[hd] + _lane_rowsum(lom)
                rb_ref[hd] = rb_new
                rmax = jnp.max(rb_new) if rmax is None else jnp.maximum(rmax, jnp.max(rb_new))
            return n + 1, rmax

        lax.while_loop(cond, body, (jnp.int32(2), rmax))
        for hd, cols in heads:
            o_ref[0, rows(qi), cols] = acc_ref[hd].astype(o_ref.dtype)

    q_tile(0, True)

    def q_body(qi, carry):
        q_tile(qi, False)
        return carry

    lax.fori_loop(1, nq, q_body, 0)


def _sbattn(proj3, tri, *, tq, col0, heads_per_step):
    b, s_len, _ = proj3.shape
    h = N_HEADS_SB
    hp = heads_per_step
    kern = functools.partial(_sbattn_kernel, tq=tq)
    blk = lambda off: pl.BlockSpec((1, s_len, hp * LANES), lambda bi, hi: (bi, 0, off // hp + hi))
    return pl.pallas_call(
        kern,
        grid=(b, h // hp),
        in_specs=[
            blk(col0), blk(col0 + h), blk(col0 + 2 * h),
            pl.BlockSpec((tq, tq), lambda bi, hi: (0, 0)),
        ],
        out_specs=pl.BlockSpec((1, s_len, hp * LANES), lambda bi, hi: (bi, 0, hi)),
        out_shape=jax.ShapeDtypeStruct((b, s_len, h * SB_HEAD_DIM), BF16),
        scratch_shapes=[
            pltpu.VMEM((hp, tq, LANES), F32),
            pltpu.VMEM((hp, tq, SB_HEAD_DIM), F32),
        ],
        compiler_params=_cparams(("parallel", "parallel")),
        name="sbattn",
    )(proj3, proj3, proj3, tri)


def _split_bf16(a):
    hi = a.astype(BF16)
    return hi, (a - hi.astype(F32)).astype(BF16)


def _mix_kernel(od_ref, os_ref, g0_ref, g1_ref, x_ref, wd_ref, ws_ref, wo_ref, gm_ref,
                wr_hi_ref, wr_lo_ref, br_ref, ltri_ref, x1_ref, hn_ref, route_ref, cnt_ref):
    @pl.when(pl.program_id(0) == 0)
    def _():
        cnt_ref[...] = jnp.zeros(cnt_ref.shape, F32)

    yd = jnp.dot(od_ref[...], wd_ref[...], preferred_element_type=F32)
    ys = jnp.dot(os_ref[...], ws_ref[...], preferred_element_type=F32)
    mixed = g0_ref[...].astype(F32) * yd + g1_ref[...].astype(F32) * ys
    x1 = x_ref[...] + jnp.dot(mixed.astype(BF16), wo_ref[...], preferred_element_type=F32)
    x1_ref[...] = x1
    hn = _rms(x1, gm_ref[...])
    hn_hi, hn_lo = _split_bf16(hn)
    tm = hn.shape[0]
    hn_f = hn_hi.astype(F32)
    for c in range(SLAB_PITCH):
        part = hn_f[:, c * LANES:(c + 1) * LANES] if c < SLAB_ROWS else jnp.zeros((tm, LANES), F32)
        hn_ref[pl.ds(c, tm, stride=SLAB_PITCH), :] = part

    quad = jnp.dot(jnp.concatenate([hn_hi, hn_lo], axis=0),
                   jnp.concatenate([wr_hi_ref[...], wr_lo_ref[...]], axis=1), preferred_element_type=F32)
    logits = (quad[:tm, :LANES] + quad[:tm, LANES:]) + (quad[tm:, :LANES] + quad[tm:, LANES:]) + br_ref[...]
    lane = lax.broadcasted_iota(jnp.int32, logits.shape, 1)
    big = jnp.int32(LANES)

    def first_argmax(vals, vmax):
        return jnp.min(jnp.where(vals == vmax, lane, big), axis=-1, keepdims=True)

    is_group = (lane >= N_EXPERTS) & (lane < N_EXPERTS + N_GROUPS)
    gl = jnp.where(is_group, logits, -jnp.inf)
    gmax = jnp.max(gl, axis=-1, keepdims=True)
    gexp = jnp.exp(gl - gmax)
    probs = gexp / jnp.sum(gexp, axis=-1, keepdims=True)
    g_val = jnp.max(probs, axis=-1, keepdims=True)
    g_idx = first_argmax(jnp.where(is_group, probs, -jnp.inf), g_val) - N_EXPERTS

    in_group = (lane >= g_idx * EXPERTS_PER_GROUP) & (lane < (g_idx + 1) * EXPERTS_PER_GROUP)
    el = jnp.where(in_group, logits, -jnp.inf)
    v1 = jnp.max(el, axis=-1, keepdims=True)
    i1 = first_argmax(el, v1)
    el2 = jnp.where(lane == i1, -jnp.inf, el)
    v2 = jnp.max(el2, axis=-1, keepdims=True)
    i2 = first_argmax(el2, v2)
    e2 = jnp.exp(v2 - v1)
    den = 1.0 + e2
    w1 = (1.0 / den) * g_val
    w2 = (e2 / den) * g_val

    pick1 = lane == i1
    pick2 = lane == i2
    onehot = jnp.where(pick1 | pick2, 1.0, 0.0).astype(BF16)
    seen = jnp.dot(ltri_ref[...], onehot, preferred_element_type=F32) + cnt_ref[...]
    rank1 = jnp.sum(jnp.where(pick1, seen, 0.0), axis=-1, keepdims=True) - 1.0
    rank2 = jnp.sum(jnp.where(pick2, seen, 0.0), axis=-1, keepdims=True) - 1.0
    cnt_ref[...] = seen[seen.shape[0] - 1:, :]
    cols = (w1, w2, i1.astype(F32), i2.astype(F32), rank1, rank2)
    route = jnp.zeros(logits.shape, F32)
    for c, val in enumerate(cols):
        route = jnp.where(lane == c, val, route)
    route_ref[...] = route


def _mix(o_diff, o_sb, proj, x2, w_brd, w_brs, w_out, g_moe, wr_hi, wr_lo, b_r, *, tm, gate_col0):
    t_tok, d = x2.shape
    wd_in = o_diff.shape[1]
    ws_in = o_sb.shape[1]
    gblk = gate_col0 // d
    r = np.arange(tm)
    ltri = jnp.asarray(r[None, :] <= r[:, None], dtype=BF16)
    const = lambda shape: pl.BlockSpec(shape, lambda i: (0, 0), pipeline_mode=pl.Buffered(1))
    return pl.pallas_call(
        _mix_kernel,
        grid=(t_tok // tm,),
        in_specs=[
            pl.BlockSpec((tm, wd_in), lambda i: (i, 0)),
            pl.BlockSpec((tm, ws_in), lambda i: (i, 0)),
            pl.BlockSpec((tm, d), lambda i: (i, gblk)),
            pl.BlockSpec((tm, d), lambda i: (i, gblk + 1)),
            pl.BlockSpec((tm, d), lambda i: (i, 0)),
            const(w_brd.shape), const(w_brs.shape), const(w_out.shape),
            const(g_moe.shape), const(wr_hi.shape), const(wr_lo.shape), const(b_r.shape),
            const(ltri.shape),
        ],
        out_specs=[
            pl.BlockSpec((tm, d), lambda i: (i, 0)),
            pl.BlockSpec((tm * SLAB_PITCH, LANES), lambda i: (i, 0)),
            pl.BlockSpec((tm, LANES), lambda i: (i, 0)),
            pl.BlockSpec((1, LANES), lambda i: (0, 0)),
        ],
        out_shape=[
            jax.ShapeDtypeStruct((t_tok, d), F32),
            jax.ShapeDtypeStruct((t_tok * SLAB_PITCH, LANES), F32),
            jax.ShapeDtypeStruct((t_tok, LANES), F32),
            jax.ShapeDtypeStruct((1, LANES), F32),
        ],
        compiler_params=_cparams(("arbitrary",)),
        name="mix",
    )(o_diff, o_sb, proj, proj, x2, w_brd, w_brs, w_out, g_moe, wr_hi, wr_lo, b_r, ltri)


def _gather_rows(src_hbm, idx_ref, base, dst, sem, n_rows):
    for r in range(n_rows):
        row = idx_ref[base + r]
        pltpu.make_async_copy(src_hbm.at[pl.ds(row, 1), :], dst.at[pl.ds(r, 1), :], sem).start(priority=r % 2)


def _wait_rows(src_hbm, dst, sem):
    pltpu.make_async_copy(src_hbm.at[pl.ds(0, dst.shape[0]), :], dst, sem).wait()


def _gather_slabs(src_hbm, idx_ref, base, dst, sem, n_items):
    for r in range(n_items):
        tok = idx_ref[base + r]
        pltpu.make_async_copy(src_hbm.at[pl.ds(tok * SLAB_PITCH, SLAB_ROWS), :],
                              dst.at[pl.ds(r * SLAB_PITCH, SLAB_ROWS), :], sem).start(priority=r % 2)


def _wait_slabs(src_hbm, dst, sem, n_items):
    n = n_items * SLAB_ROWS
    pltpu.make_async_copy(src_hbm.at[pl.ds(0, n), :], dst.at[pl.ds(0, n), :], sem).wait()


def _experts_kernel(te_ref, nused_ref, src_ref, hn_hbm, wg_ref, wu_ref, wd_ref, y_ref, xbuf, sem):
    i = pl.program_id(0)
    tm = y_ref.shape[0]
    n_used = nused_ref[0]
    slot = i % 2

    @pl.when(i == 0)
    def _():
        _gather_slabs(hn_hbm, src_ref, 0, xbuf.at[0], sem.at[0], tm)

    @pl.when(i + 1 < n_used)
    def _():
        _gather_slabs(hn_hbm, src_ref, (i + 1) * tm, xbuf.at[1 - slot], sem.at[1 - slot], tm)

    @pl.when(i < n_used)
    def _():
        _wait_slabs(hn_hbm, xbuf.at[slot], sem.at[slot], tm)
        x = jnp.concatenate([xbuf[slot, pl.ds(c, tm, stride=SLAB_PITCH), :] for c in range(SLAB_ROWS)],
                            axis=1).astype(BF16)
        g = jnp.dot(x, wg_ref[0], preferred_element_type=F32)
        u = jnp.dot(x, wu_ref[0], preferred_element_type=F32)
        a = (jax.nn.silu(g) * u).astype(BF16)
        y_ref[...] = jnp.dot(a, wd_ref[0], preferred_element_type=F32)

    @pl.when(i >= n_used)
    def _():
        y_ref[...] = jnp.zeros(y_ref.shape, y_ref.dtype)


def _experts(tile_expert, n_used, src, hn_slabs, w_g, w_u, w_d, *, tm):
    n_rows = src.shape[0]
    n_e, d, ff = w_g.shape
    assert d == SLAB_ROWS * LANES
    grid_spec = pltpu.PrefetchScalarGridSpec(
        num_scalar_prefetch=3,
        grid=(n_rows // tm,),
        in_specs=[
            pl.BlockSpec(memory_space=pl.ANY),
            pl.BlockSpec((1, d, ff), lambda i, te, nu, sr: (te[i], 0, 0)),
            pl.BlockSpec((1, d, ff), lambda i, te, nu, sr: (te[i], 0, 0)),
            pl.BlockSpec((1, ff, d), lambda i, te, nu, sr: (te[i], 0, 0)),
        ],
        out_specs=pl.BlockSpec((tm, d), lambda i, te, nu, sr: (i, 0)),
        scratch_shapes=[pltpu.VMEM((2, tm * SLAB_PITCH, LANES), F32), pltpu.SemaphoreType.DMA((2,))],
    )
    return pl.pallas_call(
        _experts_kernel,
        grid_spec=grid_spec,
        out_shape=jax.ShapeDtypeStruct((n_rows, d), F32),
        compiler_params=pltpu.CompilerParams(dimension_semantics=("arbitrary",),
                                             vmem_limit_bytes=VMEM_LIMIT, disable_bounds_checks=True),
        name="experts",
    )(tile_expert, n_used, src, hn_slabs, w_g, w_u, w_d)


def _combine_kernel(d1_ref, d2_ref, y_hbm, x1_ref, route_ref, gf_ref, o_ref, ybuf, sem):
    i = pl.program_id(0)
    n = pl.num_programs(0)
    tm = x1_ref.shape[0]
    slot = i % 2

    def gather(tile, s):
        _gather_rows(y_hbm, d1_ref, tile * tm, ybuf.at[s, 0], sem.at[s], tm)
        _gather_rows(y_hbm, d2_ref, tile * tm, ybuf.at[s, 1], sem.at[s], tm)

    @pl.when(i == 0)
    def _():
        gather(0, 0)

    @pl.when(i + 1 < n)
    def _():
        gather(i + 1, 1 - slot)

    _wait_rows(y_hbm, ybuf.at[slot, 0], sem.at[slot])
    _wait_rows(y_hbm, ybuf.at[slot, 1], sem.at[slot])
    route = route_ref[...]
    x2 = x1_ref[...] + route[:, 0:1] * ybuf[slot, 0] + route[:, 1:2] * ybuf[slot, 1]
    o_ref[...] = _rms(x2, gf_ref[...])


def _combine(d1, d2, y, x1, route, g_final, *, tm):
    t_tok, d = x1.shape
    grid_spec = pltpu.PrefetchScalarGridSpec(
        num_scalar_prefetch=2,
        grid=(t_tok // tm,),
        in_specs=[
            pl.BlockSpec(memory_space=pl.ANY),
            pl.BlockSpec((tm, d), lambda i, a, b: (i, 0)),
            pl.BlockSpec((tm, LANES), lambda i, a, b: (i, 0)),
            pl.BlockSpec((1, d), lambda i, a, b: (0, 0)),
        ],
        out_specs=pl.BlockSpec((tm, d), lambda i, a, b: (i, 0)),
        scratch_shapes=[pltpu.VMEM((2, 2, tm, d), F32), pltpu.SemaphoreType.DMA((2,))],
    )
    return pl.pallas_call(
        _combine_kernel,
        grid_spec=grid_spec,
        out_shape=jax.ShapeDtypeStruct((t_tok, d), F32),
        compiler_params=pltpu.CompilerParams(dimension_semantics=("arbitrary",),
                                             vmem_limit_bytes=VMEM_LIMIT, disable_bounds_checks=True),
        name="combine",
    )(d1, d2, y, x1, route, g_final)


def _pair_rows_kernel(route_ref, off_ref, d_ref):
    route = route_ref[...]
    off = off_ref[...]
    lane = lax.broadcasted_iota(jnp.int32, route.shape, 1)
    lane_f = lane.astype(F32)

    def row_of(e_col, rank_col):
        first = jnp.sum(jnp.where(lane_f == route[:, e_col:e_col + 1], off, 0.0), axis=-1, keepdims=True)
        return first + route[:, rank_col:rank_col + 1]

    d = jnp.where(lane == 0, row_of(2, 4), jnp.where(lane == 1, row_of(3, 5), 0.0))
    d_ref[...] = d.astype(jnp.int32)


def _pair_rows(route, off_row, *, tm):
    t_tok = route.shape[0]
    return pl.pallas_call(
        _pair_rows_kernel,
        grid=(t_tok // tm,),
        in_specs=[pl.BlockSpec((tm, LANES), lambda i: (i, 0)), pl.BlockSpec((1, LANES), lambda i: (0, 0))],
        out_specs=pl.BlockSpec((tm, LANES), lambda i: (i, 0)),
        out_shape=jax.ShapeDtypeStruct((t_tok, LANES), jnp.int32),
        compiler_params=_cparams(("parallel",)),
        name="pair_rows",
    )(route, off_row)


def _dispatch_plan(route, counts, *, tm, n_tiles):
    t_tok = route.shape[0]
    cnt = counts[0, :N_EXPERTS].astype(jnp.int32)
    tiles_per_e = (cnt + tm - 1) // tm
    tile_end = jnp.cumsum(tiles_per_e)
    off = (tile_end - tiles_per_e) * tm
    off_row = jnp.pad(off.astype(F32), (0, LANES - N_EXPERTS)).reshape(1, LANES)
    pair_rows = _pair_rows(route, off_row, tm=_tile(t_tok, 2048))
    d1 = pair_rows[:, 0]
    d2 = pair_rows[:, 1]
    tok = jnp.arange(t_tok, dtype=jnp.int32)
    src = jnp.zeros((n_tiles * tm,), jnp.int32).at[jnp.concatenate([d1, d2])].add(
        jnp.concatenate([tok, tok]))
    tile_ids = jnp.arange(n_tiles, dtype=jnp.int32)
    tile_expert = jnp.minimum(
        jnp.sum((tile_end[None, :] <= tile_ids[:, None]).astype(jnp.int32), axis=1), N_EXPERTS - 1)
    return d1, d2, src, tile_expert, tile_end[-1:].astype(jnp.int32)


def _tile(n, pref):
    t = min(n, pref)
    assert n % t == 0, (n, t)
    return t


def kernel(x, positions, g_mix, w_in, w_gate, lambda_params, g_subln, w_br_diff, w_br_sb, w_out,
           g_moe, w_group, b_group, w_router, b_router, w_up_gate, w_up, w_down, g_final):
    b, s_len, d = x.shape
    t_tok = b * s_len
    depth = w_in.shape[0]
    assert depth == 1
    l = 0
    x2 = x.reshape(t_tok, d)

    inv_freq = ROPE_THETA ** (-jnp.arange(0, ROT_DIM, 2, dtype=F32) / ROT_DIM)
    lane = np.arange(LANES)
    invf = inv_freq[(lane % DIFF_HEAD_DIM) % (ROT_DIM // 2)].reshape(1, LANES)
    posf = positions.astype(F32).reshape(t_tok, 1)

    in_w = w_in.shape[2]
    w_cat = jnp.concatenate([w_in[l].astype(BF16), w_gate[l].astype(BF16)], axis=1)
    tn = 2048
    qk_w = N_HEADS_DIFF * 2 * DIFF_HEAD_DIM
    sbq_col0 = 2 * qk_w + N_HEADS_DIFF * DIFF_V_DIM
    assert (2 * qk_w) % tn == 0 and (in_w - 2 * qk_w) % tn == 0
    proj = _proj(x2, posf, invf, g_mix[l].reshape(1, d), w_cat,
                 tm=_tile(t_tok, 512), tn=tn,
                 n_rope_tiles=2 * qk_w // tn, n_plain_tiles=(in_w - 2 * qk_w) // tn,
                 dq_cols=(0, qk_w), sbq_cols=(sbq_col0, sbq_col0 + N_HEADS_SB * SB_HEAD_DIM))
    proj3 = proj.reshape(b, s_len, proj.shape[1])

    o_diff = _dattn(proj3, lambda_params[l], g_subln[l].reshape(1, DIFF_V_DIM), tq=_tile(s_len, 512),
                    heads_per_step=2)
    tq = _tile(s_len, 256)
    r = np.arange(tq)
    tri = jnp.asarray(r[:, None] > r[None, :], dtype=BF16)
    sb_col0 = (2 * qk_w + N_HEADS_DIFF * DIFF_V_DIM) // LANES
    o_sb = _sbattn(proj3, tri, tq=tq, col0=sb_col0, heads_per_step=4)

    w_r = jnp.concatenate([w_router[l], w_group[l]], axis=1)
    w_r = jnp.pad(w_r, ((0, 0), (0, LANES - w_r.shape[1])))
    wr_hi = w_r.astype(BF16)
    wr_lo = (w_r - wr_hi.astype(F32)).astype(BF16)
    b_r = jnp.pad(jnp.concatenate([b_router[l], b_group[l]]), (0, LANES - N_EXPERTS - N_GROUPS))
    x1, hn, route, counts = _mix(
        o_diff.reshape(t_tok, -1), o_sb.reshape(t_tok, -1), proj, x2,
        w_br_diff[l].astype(BF16), w_br_sb[l].astype(BF16), w_out[l].astype(BF16),
        g_moe[l].reshape(1, d), wr_hi, wr_lo, b_r.reshape(1, LANES),
        tm=_tile(t_tok, 256), gate_col0=in_w)

    tm_e = _tile(t_tok, 512)
    n_tiles = 2 * t_tok // tm_e + N_EXPERTS
    d1, d2, src, tile_expert, n_used = _dispatch_plan(route, counts, tm=tm_e, n_tiles=n_tiles)
    y = _experts(tile_expert, n_used, src, hn,
                 w_up_gate[l].astype(BF16), w_up[l].astype(BF16), w_down[l].astype(BF16), tm=tm_e)
    out = _combine(d1, d2, y, x1, route, g_final.reshape(1, d), tm=_tile(t_tok, 256))
    return out.reshape(b, s_len, d)
```

```python
import functools

import numpy as np
import jax
import jax.numpy as jnp
from jax import lax
from jax.experimental import pallas as pl
from jax.experimental.pallas import tpu as pltpu

F32 = jnp.float32
BF16 = jnp.bfloat16

N_HEADS_DIFF = 8
DIFF_HEAD_DIM = 64
DIFF_V_DIM = 128
N_HEADS_SB = 8
SB_HEAD_DIM = 128
ROT_DIM = 16
ROPE_THETA = 500000.0
N_GROUPS = 4
EXPERTS_PER_GROUP = 8
N_EXPERTS = 32
EPS = 1e-6
NEG_INF = float(np.finfo(np.float32).min)
SB_CUTOFF = -150.5
LAMBDA_INIT = 0.8 - 0.6 * float(np.exp(-0.3 * 0))
LOG2E = float(np.log2(np.e))

LANES = 128
MXU_N = 256
SLAB_ROWS = 16
SLAB_PITCH = 20
VMEM_LIMIT = 56 * 1024 * 1024


def _cparams(sem):
    return pltpu.CompilerParams(dimension_semantics=sem, vmem_limit_bytes=VMEM_LIMIT)


def _rms(x, g):
    return x * lax.rsqrt(jnp.mean(x * x, axis=-1, keepdims=True) + EPS) * g


def _proj_kernel(x_ref, pos_ref, invf_ref, g_ref, w_ref, o_ref, h_ref, c_ref, s1_ref, s2_ref,
                 *, n_rope_tiles, n_plain_tiles, dq_cols, sbq_cols):
    j = pl.program_id(1)
    tn = o_ref.shape[1]

    @pl.when(j == 0)
    def _():
        h_ref[...] = _rms(x_ref[...], g_ref[...]).astype(BF16)
        ang = pos_ref[...] * invf_ref[...]
        lane = lax.broadcasted_iota(jnp.int32, ang.shape, 1) % DIFF_HEAD_DIM
        cos = jnp.cos(ang)
        sin = jnp.sin(ang)
        half = ROT_DIM // 2
        c_ref[...] = jnp.where(lane < ROT_DIM, cos, 1.0)
        s1_ref[...] = jnp.where(lane < half, -sin, 0.0)
        s2_ref[...] = jnp.where((lane >= half) & (lane < ROT_DIM), sin, 0.0)

    def chunks():
        h = h_ref[...]
        for c in range(tn // MXU_N):
            cols = slice(c * MXU_N, (c + 1) * MXU_N)
            yield cols, j * tn + c * MXU_N, jnp.dot(h, w_ref[:, cols], preferred_element_type=F32)

    def scale_in(col0, col_range, value):
        return jnp.where((col0 >= col_range[0]) & (col0 < col_range[1]), value, 1.0).astype(F32)

    @pl.when(j < n_rope_tiles)
    def _():
        half = ROT_DIM // 2
        for cols, col0, acc in chunks():
            qscale = scale_in(col0, dq_cols, DIFF_HEAD_DIM ** -0.5 * LOG2E)
            parts = []
            for c in range(MXU_N // LANES):
                t = acc[:, c * LANES:(c + 1) * LANES]
                up = pltpu.roll(t, LANES - half, axis=1)
                dn = pltpu.roll(t, half, axis=1)
                parts.append((t * c_ref[...] + up * s1_ref[...] + dn * s2_ref[...]) * qscale)
            o_ref[:, cols] = jnp.concatenate(parts, axis=1).astype(o_ref.dtype)

    @pl.when((j >= n_rope_tiles) & (j < n_rope_tiles + n_plain_tiles))
    def _():
        for cols, col0, acc in chunks():
            scale = scale_in(col0, sbq_cols, SB_HEAD_DIM ** -0.5 * LOG2E)
            o_ref[:, cols] = (acc * scale).astype(o_ref.dtype)

    @pl.when(j >= n_rope_tiles + n_plain_tiles)
    def _():
        for cols, _, acc in chunks():
            o_ref[:, cols] = jax.nn.sigmoid(acc).astype(o_ref.dtype)


def _proj(x2, posf, invf, g_mix, w_cat, *, tm, tn, n_rope_tiles, n_plain_tiles, dq_cols, sbq_cols):
    t_tok, d = x2.shape
    n = w_cat.shape[1]
    kern = functools.partial(_proj_kernel, n_rope_tiles=n_rope_tiles, n_plain_tiles=n_plain_tiles,
                             dq_cols=dq_cols, sbq_cols=sbq_cols)
    return pl.pallas_call(
        kern,
        grid=(t_tok // tm, n // tn),
        in_specs=[
            pl.BlockSpec((tm, d), lambda i, j: (i, 0)),
            pl.BlockSpec((tm, 1), lambda i, j: (i, 0)),
            pl.BlockSpec((1, LANES), lambda i, j: (0, 0)),
            pl.BlockSpec((1, d), lambda i, j: (0, 0)),
            pl.BlockSpec((d, tn), lambda i, j: (0, j)),
        ],
        out_specs=pl.BlockSpec((tm, tn), lambda i, j: (i, j)),
        out_shape=jax.ShapeDtypeStruct((t_tok, n), BF16),
        scratch_shapes=[
            pltpu.VMEM((tm, d), BF16),
            pltpu.VMEM((tm, LANES), F32),
            pltpu.VMEM((tm, LANES), F32),
            pltpu.VMEM((tm, LANES), F32),
        ],
        compiler_params=_cparams(("parallel", "arbitrary")),
        name="proj",
    )(x2, posf, invf, g_mix, w_cat)


def _dattn_kernel(q_ref, k_ref, v_ref, lp_ref, g_ref, o_ref, s_ref, ml_ref, mb_ref, acc_ref, vx_ref, *, tq):
    s_len = q_ref.shape[1]
    n_heads = q_ref.shape[2] // LANES
    nq = s_len // tq
    nc = tq // LANES
    lp = lp_ref[...]
    lam = (jnp.exp(jnp.sum(lp[0:1] * lp[1:2], axis=-1, keepdims=True))
           - jnp.exp(jnp.sum(lp[2:3] * lp[3:4], axis=-1, keepdims=True)) + LAMBDA_INIT)
    lane = lax.broadcasted_iota(jnp.int32, (tq, LANES), 1)
    row = lax.broadcasted_iota(jnp.int32, (2 * tq, tq), 0) % tq
    col = lax.broadcasted_iota(jnp.int32, (2 * tq, tq), 1)
    causal = col <= row
    heads = [(hd, slice(hd * LANES, (hd + 1) * LANES)) for hd in range(n_heads)]
    for hd, cols in heads:
        vx_ref[hd, :, :DIFF_V_DIM] = v_ref[0, :, cols]
        vx_ref[hd, :, DIFF_V_DIM:] = jnp.ones((s_len, LANES), vx_ref.dtype)

    def rows(i):
        return pl.ds(pl.multiple_of(i * tq, tq), tq)

    def q_body(qi, carry):
        qqs = []
        for hd, cols in heads:
            q = q_ref[0, rows(qi), cols]
            zero = jnp.zeros_like(q)
            qqs.append(jnp.concatenate([jnp.where(lane < DIFF_HEAD_DIM, q, zero),
                                        jnp.where(lane >= DIFF_HEAD_DIM, q, zero)], axis=0))
        ml_ref[...] = jnp.full(ml_ref.shape, NEG_INF, F32)

        def scores(kj, masked):
            for hd, cols in heads:
                k = k_ref[0, rows(kj), cols]
                s = lax.dot_general(qqs[hd], k, (((1,), (1,)), ((), ())), preferred_element_type=F32)
                if masked:
                    s = jnp.where(causal, s, NEG_INF)
                s_ref[hd, kj] = s
                m = ml_ref[hd]
                for c in range(nc):
                    m = jnp.maximum(m, s[:, c * LANES:(c + 1) * LANES])
                ml_ref[hd] = m

        def s_body(kj, c):
            scores(kj, False)
            return c

        lax.fori_loop(0, qi, s_body, 0)
        scores(qi, True)

        for hd, _ in heads:
            mb_ref[hd] = jnp.broadcast_to(jnp.max(ml_ref[hd], axis=-1, keepdims=True), mb_ref.shape[1:])
        acc_ref[...] = jnp.zeros(acc_ref.shape, F32)

        def av_body(kj, c):
            for hd, _ in heads:
                mb = mb_ref[hd]
                p = jnp.concatenate(
                    [jnp.exp2(s_ref[hd, kj, :, c * LANES:(c + 1) * LANES] - mb) for c in range(nc)], axis=1)
                acc_ref[hd] += jnp.dot(p.astype(BF16), vx_ref[hd, rows(kj), :], preferred_element_type=F32)
            return c

        lax.fori_loop(0, qi + 1, av_body, 0)

        for hd, cols in heads:
            o = acc_ref[hd, :, :DIFF_V_DIM] / acc_ref[hd, :, DIFF_V_DIM:]
            od = o[:tq] - lam * o[tq:]
            od = _rms(od, g_ref[...]) * (1.0 - LAMBDA_INIT)
            o_ref[0, rows(qi), cols] = od.astype(o_ref.dtype)
        return carry

    lax.fori_loop(0, nq, q_body, 0)


def _dattn(proj3, lam_params, g_subln, *, tq, heads_per_step):
    b, s_len, _ = proj3.shape
    h = N_HEADS_DIFF
    hp = heads_per_step
    kern = functools.partial(_dattn_kernel, tq=tq)
    blk = lambda off: pl.BlockSpec((1, s_len, hp * LANES), lambda bi, hi: (bi, 0, off // hp + hi))
    return pl.pallas_call(
        kern,
        grid=(b, h // hp),
        in_specs=[
            blk(0), blk(h), blk(2 * h),
            pl.BlockSpec(lam_params.shape, lambda bi, hi: (0, 0)),
            pl.BlockSpec((1, DIFF_V_DIM), lambda bi, hi: (0, 0)),
        ],
        out_specs=pl.BlockSpec((1, s_len, hp * LANES), lambda bi, hi: (bi, 0, hi)),
        out_shape=jax.ShapeDtypeStruct((b, s_len, h * DIFF_V_DIM), BF16),
        scratch_shapes=[
            pltpu.VMEM((hp, s_len // tq, 2 * tq, tq), F32),
            pltpu.VMEM((hp, 2 * tq, LANES), F32),
            pltpu.VMEM((hp, 2 * tq, LANES), F32),
            pltpu.VMEM((hp, 2 * tq, DIFF_V_DIM + LANES), F32),
            pltpu.VMEM((hp, s_len, DIFF_V_DIM + LANES), BF16),
        ],
        compiler_params=_cparams(("parallel", "parallel")),
        name="dattn",
    )(proj3, proj3, proj3, lam_params, g_subln)


def _sb_terms(q, k, u, strict):
    z = lax.dot_general(q, k, (((1,), (1,)), ((), ())), preferred_element_type=F32)
    l1p = jnp.log2(1.0 + jnp.exp2(-jnp.abs(z)))
    lom = jnp.minimum(-z, 0.0) - l1p
    log_beta = lom + z
    if strict is not None:
        lom = jnp.where(strict, lom, 0.0)
    hi = lom.astype(BF16)
    lo = (lom - hi.astype(F32)).astype(BF16)
    suffix = jnp.dot(hi, u, preferred_element_type=F32) + jnp.dot(lo, u, preferred_element_type=F32)
    return log_beta, lom, suffix


def _lane_rowsum(x):
    return jnp.broadcast_to(jnp.sum(x, axis=-1, keepdims=True), (x.shape[0], LANES))


def _exp_shifted(x, rb):
    nc = x.shape[1] // LANES
    return jnp.concatenate([jnp.exp2(x[:, c * LANES:(c + 1) * LANES] + rb) for c in range(nc)], axis=1)


def _sbattn_kernel(q_ref, k_ref, v_ref, u_ref, o_ref, rb_ref, acc_ref, *, tq):
    s_len = q_ref.shape[1]
    n_heads = q_ref.shape[2] // LANES
    nq = s_len // tq
    row = lax.broadcasted_iota(jnp.int32, (tq, tq), 0)
    col = lax.broadcasted_iota(jnp.int32, (tq, tq), 1)
    strict = col < row
    heads = [(hd, slice(hd * LANES, (hd + 1) * LANES)) for hd in range(n_heads)]

    def rows(i):
        return pl.ds(pl.multiple_of(i * tq, tq), tq)

    def q_tile(qi, first):
        u = u_ref[...]
        rmax = None
        for hd, cols in heads:
            q = q_ref[0, rows(qi), cols]
            log_beta, lom, suffix = _sb_terms(q, k_ref[0, rows(qi), cols], u, strict)
            a = jnp.where(strict, jnp.exp2(log_beta + suffix), 0.0)
            acc = jnp.dot(a.astype(BF16), v_ref[0, rows(qi), cols], preferred_element_type=F32)
            rb = _lane_rowsum(lom)
            if first:
                o_ref[0, rows(qi), cols] = acc.astype(o_ref.dtype)
                continue
            log_beta, lom, suffix = _sb_terms(q, k_ref[0, rows(qi - 1), cols], u, None)
            a = _exp_shifted(log_beta + suffix, rb)
            acc_ref[hd] = acc + jnp.dot(a.astype(BF16), v_ref[0, rows(qi - 1), cols],
                                        preferred_element_type=F32)
            rb = rb + _lane_rowsum(lom)
            rb_ref[hd] = rb
            rmax = jnp.max(rb) if rmax is None else jnp.maximum(rmax, jnp.max(rb))
        if first:
            return

        def cond(c):
            n, rmax = c
            return (n <= qi) & (rmax > SB_CUTOFF)

        def body(c):
            n, _ = c
            kj = qi - n
            rmax = None
            for hd, cols in heads:
                log_beta, lom, suffix = _sb_terms(q_ref[0, rows(qi), cols], k_ref[0, rows(kj), cols],
                                                  u_ref[...], None)
                a = _exp_shifted(log_beta + suffix, rb_ref[hd])
                acc_ref[hd] += jnp.dot(a.astype(BF16), v_ref[0, rows(kj), cols], preferred_element_type=F32)
                rb_new = rb_ref[hd] + _lane_rowsum(lom)
                rb_ref[hd] = rb_new
                rmax = jnp.max(rb_new) if rmax is None else jnp.maximum(rmax, jnp.max(rb_new))
            return n + 1, rmax

        lax.while_loop(cond, body, (jnp.int32(2), rmax))
        for hd, cols in heads:
            o_ref[0, rows(qi), cols] = acc_ref[hd].astype(o_ref.dtype)

    q_tile(0, True)

    def q_body(qi, carry):
        q_tile(qi, False)
        return carry

    lax.fori_loop(1, nq, q_body, 0)


def _sbattn(proj3, tri, *, tq, col0, heads_per_step):
    b, s_len, _ = proj3.shape
    h = N_HEADS_SB
    hp = heads_per_step
    kern = functools.partial(_sbattn_kernel, tq=tq)
    blk = lambda off: pl.BlockSpec((1, s_len, hp * LANES), lambda bi, hi: (bi, 0, off // hp + hi))
    return pl.pallas_call(
        kern,
        grid=(b, h // hp),
        in_specs=[
            blk(col0), blk(col0 + h), blk(col0 + 2 * h),
            pl.BlockSpec((tq, tq), lambda bi, hi: (0, 0)),
        ],
        out_specs=pl.BlockSpec((1, s_len, hp * LANES), lambda bi, hi: (bi, 0, hi)),
        out_shape=jax.ShapeDtypeStruct((b, s_len, h * SB_HEAD_DIM), BF16),
        scratch_shapes=[
            pltpu.VMEM((hp, tq, LANES), F32),
            pltpu.VMEM((hp, tq, SB_HEAD_DIM), F32),
        ],
        compiler_params=_cparams(("parallel", "parallel")),
        name="sbattn",
    )(proj3, proj3, proj3, tri)


def _split_bf16(a):
    hi = a.astype(BF16)
    return hi, (a - hi.astype(F32)).astype(BF16)


def _mix_kernel(od_ref, os_ref, g0_ref, g1_ref, x_ref, wd_ref, ws_ref, wo_ref, gm_ref,
                wr_hi_ref, wr_lo_ref, br_ref, ltri_ref, x1_ref, hn_ref, route_ref, cnt_ref):
    @pl.when(pl.program_id(0) == 0)
    def _():
        cnt_ref[...] = jnp.zeros(cnt_ref.shape, F32)

    yd = jnp.dot(od_ref[...], wd_ref[...], preferred_element_type=F32)
    ys = jnp.dot(os_ref[...], ws_ref[...], preferred_element_type=F32)
    mixed = g0_ref[...].astype(F32) * yd + g1_ref[...].astype(F32) * ys
    x1 = x_ref[...] + jnp.dot(mixed.astype(BF16), wo_ref[...], preferred_element_type=F32)
    x1_ref[...] = x1
    hn = _rms(x1, gm_ref[...])
    hn_hi, hn_lo = _split_bf16(hn)
    tm = hn.shape[0]
    hn_f = hn_hi.astype(F32)
    for c in range(SLAB_PITCH):
        part = hn_f[:, c * LANES:(c + 1) * LANES] if c < SLAB_ROWS else jnp.zeros((tm, LANES), F32)
        hn_ref[pl.ds(c, tm, stride=SLAB_PITCH), :] = part

    quad = jnp.dot(jnp.concatenate([hn_hi, hn_lo], axis=0),
                   jnp.concatenate([wr_hi_ref[...], wr_lo_ref[...]], axis=1), preferred_element_type=F32)
    logits = (quad[:tm, :LANES] + quad[:tm, LANES:]) + (quad[tm:, :LANES] + quad[tm:, LANES:]) + br_ref[...]
    lane = lax.broadcasted_iota(jnp.int32, logits.shape, 1)
    big = jnp.int32(LANES)

    def first_argmax(vals, vmax):
        return jnp.min(jnp.where(vals == vmax, lane, big), axis=-1, keepdims=True)

    is_group = (lane >= N_EXPERTS) & (lane < N_EXPERTS + N_GROUPS)
    gl = jnp.where(is_group, logits, -jnp.inf)
    gmax = jnp.max(gl, axis=-1, keepdims=True)
    gexp = jnp.exp(gl - gmax)
    probs = gexp / jnp.sum(gexp, axis=-1, keepdims=True)
    g_val = jnp.max(probs, axis=-1, keepdims=True)
    g_idx = first_argmax(jnp.where(is_group, probs, -jnp.inf), g_val) - N_EXPERTS

    in_group = (lane >= g_idx * EXPERTS_PER_GROUP) & (lane < (g_idx + 1) * EXPERTS_PER_GROUP)
    el = jnp.where(in_group, logits, -jnp.inf)
    v1 = jnp.max(el, axis=-1, keepdims=True)
    i1 = first_argmax(el, v1)
    el2 = jnp.where(lane == i1, -jnp.inf, el)
    v2 = jnp.max(el2, axis=-1, keepdims=True)
    i2 = first_argmax(el2, v2)
    e2 = jnp.exp(v2 - v1)
    den = 1.0 + e2
    w1 = (1.0 / den) * g_val
    w2 = (e2 / den) * g_val

    pick1 = lane == i1
    pick2 = lane == i2
    onehot = jnp.where(pick1 | pick2, 1.0, 0.0).astype(BF16)
    seen = jnp.dot(ltri_ref[...], onehot, preferred_element_type=F32) + cnt_ref[...]
    rank1 = jnp.sum(jnp.where(pick1, seen, 0.0), axis=-1, keepdims=True) - 1.0
    rank2 = jnp.sum(jnp.where(pick2, seen, 0.0), axis=-1, keepdims=True) - 1.0
    cnt_ref[...] = seen[seen.shape[0] - 1:, :]
    cols = (w1, w2, i1.astype(F32), i2.astype(F32), rank1, rank2)
    route = jnp.zeros(logits.shape, F32)
    for c, val in enumerate(cols):
        route = jnp.where(lane == c, val, route)
    route_ref[...] = route


def _mix(o_diff, o_sb, proj, x2, w_brd, w_brs, w_out, g_moe, wr_hi, wr_lo, b_r, *, tm, gate_col0):
    t_tok, d = x2.shape
    wd_in = o_diff.shape[1]
    ws_in = o_sb.shape[1]
    gblk = gate_col0 // d
    r = np.arange(tm)
    ltri = jnp.asarray(r[None, :] <= r[:, None], dtype=BF16)
    const = lambda shape: pl.BlockSpec(shape, lambda i: (0, 0), pipeline_mode=pl.Buffered(1))
    return pl.pallas_call(
        _mix_kernel,
        grid=(t_tok // tm,),
        in_specs=[
            pl.BlockSpec((tm, wd_in), lambda i: (i, 0)),
            pl.BlockSpec((tm, ws_in), lambda i: (i, 0)),
            pl.BlockSpec((tm, d), lambda i: (i, gblk)),
            pl.BlockSpec((tm, d), lambda i: (i, gblk + 1)),
            pl.BlockSpec((tm, d), lambda i: (i, 0)),
            const(w_brd.shape), const(w_brs.shape), const(w_out.shape),
            const(g_moe.shape), const(wr_hi.shape), const(wr_lo.shape), const(b_r.shape),
            const(ltri.shape),
        ],
        out_specs=[
            pl.BlockSpec((tm, d), lambda i: (i, 0)),
            pl.BlockSpec((tm * SLAB_PITCH, LANES), lambda i: (i, 0)),
            pl.BlockSpec((tm, LANES), lambda i: (i, 0)),
            pl.BlockSpec((1, LANES), lambda i: (0, 0)),
        ],
        out_shape=[
            jax.ShapeDtypeStruct((t_tok, d), F32),
            jax.ShapeDtypeStruct((t_tok * SLAB_PITCH, LANES), F32),
            jax.ShapeDtypeStruct((t_tok, LANES), F32),
            jax.ShapeDtypeStruct((1, LANES), F32),
        ],
        compiler_params=_cparams(("arbitrary",)),
        name="mix",
    )(o_diff, o_sb, proj, proj, x2, w_brd, w_brs, w_out, g_moe, wr_hi, wr_lo, b_r, ltri)


def _gather_rows(src_hbm, idx_ref, base, dst, sem, n_rows):
    for r in range(n_rows):
        row = idx_ref[base + r]
        pltpu.make_async_copy(src_hbm.at[pl.ds(row, 1), :], dst.at[pl.ds(r, 1), :], sem).start(priority=r % 2)


def _wait_rows(src_hbm, dst, sem):
    pltpu.make_async_copy(src_hbm.at[pl.ds(0, dst.shape[0]), :], dst, sem).wait()


def _gather_slabs(src_hbm, idx_ref, base, dst, sem, n_items, first_item=0):
    for r in range(first_item, n_items):
        tok = idx_ref[base + r]
        pltpu.make_async_copy(src_hbm.at[pl.ds(tok * SLAB_PITCH, SLAB_ROWS), :],
                              dst.at[pl.ds(r * SLAB_PITCH, SLAB_ROWS), :], sem).start(priority=r % 2)


def _wait_slabs(src_hbm, dst, sem, n_items):
    n = n_items * SLAB_ROWS
    pltpu.make_async_copy(src_hbm.at[pl.ds(0, n), :], dst.at[pl.ds(0, n), :], sem).wait()


def _experts_kernel(te_ref, nused_ref, src_ref, hn_hbm, wg_ref, wu_ref, wd_ref, y_ref,
                    xbuf, xb_ref, g_ref, a_ref, sem):
    i = pl.program_id(0)
    tm = y_ref.shape[0]
    d = y_ref.shape[1]
    n_used = nused_ref[0]
    slot = i % 2
    used = i < n_used
    n_pieces = 4

    @pl.when(i == 0)
    def _():
        _gather_slabs(hn_hbm, src_ref, 0, xbuf.at[0], sem.at[0], tm)

    def start_piece(p):
        @pl.when(i + 1 < n_used)
        def _():
            _gather_slabs(hn_hbm, src_ref, (i + 1) * tm, xbuf.at[1 - slot], sem.at[1 - slot],
                          (p + 1) * tm // n_pieces, p * tm // n_pieces)

    @pl.when(used)
    def _():
        _wait_slabs(hn_hbm, xbuf.at[slot], sem.at[slot], tm)
        xb_ref[...] = jnp.concatenate(
            [xbuf[slot, pl.ds(c, tm, stride=SLAB_PITCH), :] for c in range(SLAB_ROWS)], axis=1).astype(BF16)

    start_piece(0)

    @pl.when(used)
    def _():
        g_ref[...] = jnp.dot(xb_ref[...], wg_ref[0], preferred_element_type=F32)

    start_piece(1)

    @pl.when(used)
    def _():
        u = jnp.dot(xb_ref[...], wu_ref[0], preferred_element_type=F32)
        a_ref[...] = (jax.nn.silu(g_ref[...]) * u).astype(BF16)

    start_piece(2)

    @pl.when(used)
    def _():
        y_ref[:, :d // 2] = jnp.dot(a_ref[...], wd_ref[0, :, :d // 2], preferred_element_type=F32)

    start_piece(3)

    @pl.when(used)
    def _():
        y_ref[:, d // 2:] = jnp.dot(a_ref[...], wd_ref[0, :, d // 2:], preferred_element_type=F32)

    @pl.when(i >= n_used)
    def _():
        y_ref[...] = jnp.zeros(y_ref.shape, y_ref.dtype)


def _experts(tile_expert, n_used, src, hn_slabs, w_g, w_u, w_d, *, tm):
    n_rows = src.shape[0]
    n_e, d, ff = w_g.shape
    assert d == SLAB_ROWS * LANES
    grid_spec = pltpu.PrefetchScalarGridSpec(
        num_scalar_prefetch=3,
        grid=(n_rows // tm,),
        in_specs=[
            pl.BlockSpec(memory_space=pl.ANY),
            pl.BlockSpec((1, d, ff), lambda i, te, nu, sr: (te[i], 0, 0)),
            pl.BlockSpec((1, d, ff), lambda i, te, nu, sr: (te[i], 0, 0)),
            pl.BlockSpec((1, ff, d), lambda i, te, nu, sr: (te[i], 0, 0)),
        ],
        out_specs=pl.BlockSpec((tm, d), lambda i, te, nu, sr: (i, 0)),
        scratch_shapes=[
            pltpu.VMEM((2, tm * SLAB_PITCH, LANES), F32),
            pltpu.VMEM((tm, d), BF16),
            pltpu.VMEM((tm, ff), F32),
            pltpu.VMEM((tm, ff), BF16),
            pltpu.SemaphoreType.DMA((2,)),
        ],
    )
    return pl.pallas_call(
        _experts_kernel,
        grid_spec=grid_spec,
        out_shape=jax.ShapeDtypeStruct((n_rows, d), F32),
        compiler_params=pltpu.CompilerParams(dimension_semantics=("arbitrary",),
                                             vmem_limit_bytes=VMEM_LIMIT, disable_bounds_checks=True),
        name="experts",
    )(tile_expert, n_used, src, hn_slabs, w_g, w_u, w_d)


def _combine_kernel(d1_ref, d2_ref, y_hbm, x1_ref, route_ref, gf_ref, o_ref, ybuf, sem):
    i = pl.program_id(0)
    n = pl.num_programs(0)
    tm = x1_ref.shape[0]
    slot = i % 2

    def gather(tile, s):
        _gather_rows(y_hbm, d1_ref, tile * tm, ybuf.at[s, 0], sem.at[s], tm)
        _gather_rows(y_hbm, d2_ref, tile * tm, ybuf.at[s, 1], sem.at[s], tm)

    @pl.when(i == 0)
    def _():
        gather(0, 0)

    @pl.when(i + 1 < n)
    def _():
        gather(i + 1, 1 - slot)

    _wait_rows(y_hbm, ybuf.at[slot, 0], sem.at[slot])
    _wait_rows(y_hbm, ybuf.at[slot, 1], sem.at[slot])
    route = route_ref[...]
    x2 = x1_ref[...] + route[:, 0:1] * ybuf[slot, 0] + route[:, 1:2] * ybuf[slot, 1]
    o_ref[...] = _rms(x2, gf_ref[...])


def _combine(d1, d2, y, x1, route, g_final, *, tm):
    t_tok, d = x1.shape
    grid_spec = pltpu.PrefetchScalarGridSpec(
        num_scalar_prefetch=2,
        grid=(t_tok // tm,),
        in_specs=[
            pl.BlockSpec(memory_space=pl.ANY),
            pl.BlockSpec((tm, d), lambda i, a, b: (i, 0)),
            pl.BlockSpec((tm, LANES), lambda i, a, b: (i, 0)),
            pl.BlockSpec((1, d), lambda i, a, b: (0, 0)),
        ],
        out_specs=pl.BlockSpec((tm, d), lambda i, a, b: (i, 0)),
        scratch_shapes=[pltpu.VMEM((2, 2, tm, d), F32), pltpu.SemaphoreType.DMA((2,))],
    )
    return pl.pallas_call(
        _combine_kernel,
        grid_spec=grid_spec,
        out_shape=jax.ShapeDtypeStruct((t_tok, d), F32),
        compiler_params=pltpu.CompilerParams(dimension_semantics=("arbitrary",),
                                             vmem_limit_bytes=VMEM_LIMIT, disable_bounds_checks=True),
        name="combine",
    )(d1, d2, y, x1, route, g_final)


def _pair_rows_kernel(route_ref, off_ref, d_ref):
    route = route_ref[...]
    off = off_ref[...]
    lane = lax.broadcasted_iota(jnp.int32, route.shape, 1)
    lane_f = lane.astype(F32)

    def row_of(e_col, rank_col):
        first = jnp.sum(jnp.where(lane_f == route[:, e_col:e_col + 1], off, 0.0), axis=-1, keepdims=True)
        return first + route[:, rank_col:rank_col + 1]

    d = jnp.where(lane == 0, row_of(2, 4), jnp.where(lane == 1, row_of(3, 5), 0.0))
    d_ref[...] = d.astype(jnp.int32)


def _pair_rows(route, off_row, *, tm):
    t_tok = route.shape[0]
    return pl.pallas_call(
        _pair_rows_kernel,
        grid=(t_tok // tm,),
        in_specs=[pl.BlockSpec((tm, LANES), lambda i: (i, 0)), pl.BlockSpec((1, LANES), lambda i: (0, 0))],
        out_specs=pl.BlockSpec((tm, LANES), lambda i: (i, 0)),
        out_shape=jax.ShapeDtypeStruct((t_tok, LANES), jnp.int32),
        compiler_params=_cparams(("parallel",)),
        name="pair_rows",
    )(route, off_row)


def _dispatch_plan(route, counts, *, tm, n_tiles):
    t_tok = route.shape[0]
    cnt = counts[0, :N_EXPERTS].astype(jnp.int32)
    tiles_per_e = (cnt + tm - 1) // tm
    tile_end = jnp.cumsum(tiles_per_e)
    off = (tile_end - tiles_per_e) * tm
    off_row = jnp.pad(off.astype(F32), (0, LANES - N_EXPERTS)).reshape(1, LANES)
    pair_rows = _pair_rows(route, off_row, tm=_tile(t_tok, 2048))
    d1 = pair_rows[:, 0]
    d2 = pair_rows[:, 1]
    tok = jnp.arange(t_tok, dtype=jnp.int32)
    src = jnp.zeros((n_tiles * tm,), jnp.int32).at[jnp.concatenate([d1, d2])].add(
        jnp.concatenate([tok, tok]))
    tile_ids = jnp.arange(n_tiles, dtype=jnp.int32)
    tile_expert = jnp.minimum(
        jnp.sum((tile_end[None, :] <= tile_ids[:, None]).astype(jnp.int32), axis=1), N_EXPERTS - 1)
    return d1, d2, src, tile_expert, tile_end[-1:].astype(jnp.int32)


def _tile(n, pref):
    t = min(n, pref)
    assert n % t == 0, (n, t)
    return t


def kernel(x, positions, g_mix, w_in, w_gate, lambda_params, g_subln, w_br_diff, w_br_sb, w_out,
           g_moe, w_group, b_group, w_router, b_router, w_up_gate, w_up, w_down, g_final):
    b, s_len, d = x.shape
    t_tok = b * s_len
    depth = w_in.shape[0]
    assert depth == 1
    l = 0
    x2 = x.reshape(t_tok, d)

    inv_freq = ROPE_THETA ** (-jnp.arange(0, ROT_DIM, 2, dtype=F32) / ROT_DIM)
    lane = np.arange(LANES)
    invf = inv_freq[(lane % DIFF_HEAD_DIM) % (ROT_DIM // 2)].reshape(1, LANES)
    posf = positions.astype(F32).reshape(t_tok, 1)

    in_w = w_in.shape[2]
    w_cat = jnp.concatenate([w_in[l].astype(BF16), w_gate[l].astype(BF16)], axis=1)
    tn = 2048
    qk_w = N_HEADS_DIFF * 2 * DIFF_HEAD_DIM
    sbq_col0 = 2 * qk_w + N_HEADS_DIFF * DIFF_V_DIM
    assert (2 * qk_w) % tn == 0 and (in_w - 2 * qk_w) % tn == 0
    proj = _proj(x2, posf, invf, g_mix[l].reshape(1, d), w_cat,
                 tm=_tile(t_tok, 512), tn=tn,
                 n_rope_tiles=2 * qk_w // tn, n_plain_tiles=(in_w - 2 * qk_w) // tn,
                 dq_cols=(0, qk_w), sbq_cols=(sbq_col0, sbq_col0 + N_HEADS_SB * SB_HEAD_DIM))
    proj3 = proj.reshape(b, s_len, proj.shape[1])

    o_diff = _dattn(proj3, lambda_params[l], g_subln[l].reshape(1, DIFF_V_DIM), tq=_tile(s_len, 512),
                    heads_per_step=2)
    tq = _tile(s_len, 256)
    r = np.arange(tq)
    tri = jnp.asarray(r[:, None] > r[None, :], dtype=BF16)
    sb_col0 = (2 * qk_w + N_HEADS_DIFF * DIFF_V_DIM) // LANES
    o_sb = _sbattn(proj3, tri, tq=tq, col0=sb_col0, heads_per_step=4)

    w_r = jnp.concatenate([w_router[l], w_group[l]], axis=1)
    w_r = jnp.pad(w_r, ((0, 0), (0, LANES - w_r.shape[1])))
    wr_hi = w_r.astype(BF16)
    wr_lo = (w_r - wr_hi.astype(F32)).astype(BF16)
    b_r = jnp.pad(jnp.concatenate([b_router[l], b_group[l]]), (0, LANES - N_EXPERTS - N_GROUPS))
    x1, hn, route, counts = _mix(
        o_diff.reshape(t_tok, -1), o_sb.reshape(t_tok, -1), proj, x2,
        w_br_diff[l].astype(BF16), w_br_sb[l].astype(BF16), w_out[l].astype(BF16),
        g_moe[l].reshape(1, d), wr_hi, wr_lo, b_r.reshape(1, LANES),
        tm=_tile(t_tok, 256), gate_col0=in_w)

    tm_e = _tile(t_tok, 512)
    n_tiles = 2 * t_tok // tm_e + N_EXPERTS
    d1, d2, src, tile_expert, n_used = _dispatch_plan(route, counts, tm=tm_e, n_tiles=n_tiles)
    y = _experts(tile_expert, n_used, src, hn,
                 w_up_gate[l].astype(BF16), w_up[l].astype(BF16), w_down[l].astype(BF16), tm=tm_e)
    out = _combine(d1, d2, y, x1, route, g_final.reshape(1, d), tm=_tile(t_tok, 256))
    return out.reshape(b, s_len, d)
```

```python
import functools

import numpy as np
import jax
import jax.numpy as jnp
from jax import lax
from jax.experimental import pallas as pl
from jax.experimental.pallas import tpu as pltpu

F32 = jnp.float32
BF16 = jnp.bfloat16

N_HEADS_DIFF = 8
DIFF_HEAD_DIM = 64
DIFF_V_DIM = 128
N_HEADS_SB = 8
SB_HEAD_DIM = 128
ROT_DIM = 16
ROPE_THETA = 500000.0
N_GROUPS = 4
EXPERTS_PER_GROUP = 8
N_EXPERTS = 32
EPS = 1e-6
NEG_INF = float(np.finfo(np.float32).min)
SB_CUTOFF = -150.5
LAMBDA_INIT = 0.8 - 0.6 * float(np.exp(-0.3 * 0))
LOG2E = float(np.log2(np.e))

LANES = 128
MXU_N = 256
SLAB_ROWS = 16
SLAB_PITCH = 20
VMEM_LIMIT = 56 * 1024 * 1024


def _cparams(sem):
    return pltpu.CompilerParams(dimension_semantics=sem, vmem_limit_bytes=VMEM_LIMIT)


def _rms(x, g):
    return x * lax.rsqrt(jnp.mean(x * x, axis=-1, keepdims=True) + EPS) * g


def _proj_kernel(x_ref, pos_ref, invf_ref, g_ref, w_ref, o_ref, h_ref, c_ref, s1_ref, s2_ref,
                 *, n_rope_tiles, n_plain_tiles, dq_cols, sbq_cols):
    j = pl.program_id(1)
    tn = o_ref.shape[1]

    @pl.when(j == 0)
    def _():
        h_ref[...] = _rms(x_ref[...], g_ref[...]).astype(BF16)
        ang = pos_ref[...] * invf_ref[...]
        lane = lax.broadcasted_iota(jnp.int32, ang.shape, 1) % DIFF_HEAD_DIM
        cos = jnp.cos(ang)
        sin = jnp.sin(ang)
        half = ROT_DIM // 2
        c_ref[...] = jnp.where(lane < ROT_DIM, cos, 1.0)
        s1_ref[...] = jnp.where(lane < half, -sin, 0.0)
        s2_ref[...] = jnp.where((lane >= half) & (lane < ROT_DIM), sin, 0.0)

    def chunks():
        h = h_ref[...]
        for c in range(tn // MXU_N):
            cols = slice(c * MXU_N, (c + 1) * MXU_N)
            yield cols, j * tn + c * MXU_N, jnp.dot(h, w_ref[:, cols], preferred_element_type=F32)

    def scale_in(col0, col_range, value):
        return jnp.where((col0 >= col_range[0]) & (col0 < col_range[1]), value, 1.0).astype(F32)

    @pl.when(j < n_rope_tiles)
    def _():
        half = ROT_DIM // 2
        for cols, col0, acc in chunks():
            qscale = scale_in(col0, dq_cols, DIFF_HEAD_DIM ** -0.5 * LOG2E)
            parts = []
            for c in range(MXU_N // LANES):
                t = acc[:, c * LANES:(c + 1) * LANES]
                up = pltpu.roll(t, LANES - half, axis=1)
                dn = pltpu.roll(t, half, axis=1)
                parts.append((t * c_ref[...] + up * s1_ref[...] + dn * s2_ref[...]) * qscale)
            o_ref[:, cols] = jnp.concatenate(parts, axis=1).astype(o_ref.dtype)

    @pl.when((j >= n_rope_tiles) & (j < n_rope_tiles + n_plain_tiles))
    def _():
        for cols, col0, acc in chunks():
            scale = scale_in(col0, sbq_cols, SB_HEAD_DIM ** -0.5 * LOG2E)
            o_ref[:, cols] = (acc * scale).astype(o_ref.dtype)

    @pl.when(j >= n_rope_tiles + n_plain_tiles)
    def _():
        for cols, _, acc in chunks():
            o_ref[:, cols] = jax.nn.sigmoid(acc).astype(o_ref.dtype)


def _proj(x2, posf, invf, g_mix, w_cat, *, tm, tn, n_rope_tiles, n_plain_tiles, dq_cols, sbq_cols):
    t_tok, d = x2.shape
    n = w_cat.shape[1]
    kern = functools.partial(_proj_kernel, n_rope_tiles=n_rope_tiles, n_plain_tiles=n_plain_tiles,
                             dq_cols=dq_cols, sbq_cols=sbq_cols)
    return pl.pallas_call(
        kern,
        grid=(t_tok // tm, n // tn),
        in_specs=[
            pl.BlockSpec((tm, d), lambda i, j: (i, 0)),
            pl.BlockSpec((tm, 1), lambda i, j: (i, 0)),
            pl.BlockSpec((1, LANES), lambda i, j: (0, 0)),
            pl.BlockSpec((1, d), lambda i, j: (0, 0)),
            pl.BlockSpec((d, tn), lambda i, j: (0, j)),
        ],
        out_specs=pl.BlockSpec((tm, tn), lambda i, j: (i, j)),
        out_shape=jax.ShapeDtypeStruct((t_tok, n), BF16),
        scratch_shapes=[
            pltpu.VMEM((tm, d), BF16),
            pltpu.VMEM((tm, LANES), F32),
            pltpu.VMEM((tm, LANES), F32),
            pltpu.VMEM((tm, LANES), F32),
        ],
        compiler_params=_cparams(("parallel", "arbitrary")),
        name="proj",
    )(x2, posf, invf, g_mix, w_cat)


def _dattn_kernel(q_ref, k_ref, v_ref, lp_ref, g_ref, o_ref, s_ref, ml_ref, mb_ref, acc_ref, vx_ref, *, tq):
    s_len = q_ref.shape[1]
    n_heads = q_ref.shape[2] // LANES
    nq = s_len // tq
    nc = tq // LANES
    lp = lp_ref[...]
    lam = (jnp.exp(jnp.sum(lp[0:1] * lp[1:2], axis=-1, keepdims=True))
           - jnp.exp(jnp.sum(lp[2:3] * lp[3:4], axis=-1, keepdims=True)) + LAMBDA_INIT)
    lane = lax.broadcasted_iota(jnp.int32, (tq, LANES), 1)
    row = lax.broadcasted_iota(jnp.int32, (2 * tq, tq), 0) % tq
    col = lax.broadcasted_iota(jnp.int32, (2 * tq, tq), 1)
    causal = col <= row
    heads = [(hd, slice(hd * LANES, (hd + 1) * LANES)) for hd in range(n_heads)]
    for hd, cols in heads:
        vx_ref[hd, :, :DIFF_V_DIM] = v_ref[0, :, cols]
        vx_ref[hd, :, DIFF_V_DIM:] = jnp.ones((s_len, LANES), vx_ref.dtype)

    def rows(i):
        return pl.ds(pl.multiple_of(i * tq, tq), tq)

    def q_body(qi, carry):
        qqs = []
        for hd, cols in heads:
            q = q_ref[0, rows(qi), cols]
            zero = jnp.zeros_like(q)
            qqs.append(jnp.concatenate([jnp.where(lane < DIFF_HEAD_DIM, q, zero),
                                        jnp.where(lane >= DIFF_HEAD_DIM, q, zero)], axis=0))
        ml_ref[...] = jnp.full(ml_ref.shape, NEG_INF, F32)

        def scores(kj, masked):
            for hd, cols in heads:
                k = k_ref[0, rows(kj), cols]
                s = lax.dot_general(qqs[hd], k, (((1,), (1,)), ((), ())), preferred_element_type=F32)
                if masked:
                    s = jnp.where(causal, s, NEG_INF)
                s_ref[hd, kj] = s
                m = ml_ref[hd]
                for c in range(nc):
                    m = jnp.maximum(m, s[:, c * LANES:(c + 1) * LANES])
                ml_ref[hd] = m

        def s_body(kj, c):
            scores(kj, False)
            return c

        lax.fori_loop(0, qi, s_body, 0)
        scores(qi, True)

        for hd, _ in heads:
            mb_ref[hd] = jnp.broadcast_to(jnp.max(ml_ref[hd], axis=-1, keepdims=True), mb_ref.shape[1:])
        acc_ref[...] = jnp.zeros(acc_ref.shape, F32)

        def av_body(kj, c):
            for hd, _ in heads:
                mb = mb_ref[hd]
                p = jnp.concatenate(
                    [jnp.exp2(s_ref[hd, kj, :, c * LANES:(c + 1) * LANES] - mb) for c in range(nc)], axis=1)
                acc_ref[hd] += jnp.dot(p.astype(BF16), vx_ref[hd, rows(kj), :], preferred_element_type=F32)
            return c

        lax.fori_loop(0, qi + 1, av_body, 0)

        for hd, cols in heads:
            o = acc_ref[hd, :, :DIFF_V_DIM] / acc_ref[hd, :, DIFF_V_DIM:]
            od = o[:tq] - lam * o[tq:]
            od = _rms(od, g_ref[...]) * (1.0 - LAMBDA_INIT)
            o_ref[0, rows(qi), cols] = od.astype(o_ref.dtype)
        return carry

    lax.fori_loop(0, nq, q_body, 0)


def _dattn(proj3, lam_params, g_subln, *, tq, heads_per_step):
    b, s_len, _ = proj3.shape
    h = N_HEADS_DIFF
    hp = heads_per_step
    kern = functools.partial(_dattn_kernel, tq=tq)
    blk = lambda off: pl.BlockSpec((1, s_len, hp * LANES), lambda bi, hi: (bi, 0, off // hp + hi))
    return pl.pallas_call(
        kern,
        grid=(b, h // hp),
        in_specs=[
            blk(0), blk(h), blk(2 * h),
            pl.BlockSpec(lam_params.shape, lambda bi, hi: (0, 0)),
            pl.BlockSpec((1, DIFF_V_DIM), lambda bi, hi: (0, 0)),
        ],
        out_specs=pl.BlockSpec((1, s_len, hp * LANES), lambda bi, hi: (bi, 0, hi)),
        out_shape=jax.ShapeDtypeStruct((b, s_len, h * DIFF_V_DIM), BF16),
        scratch_shapes=[
            pltpu.VMEM((hp, s_len // tq, 2 * tq, tq), F32),
            pltpu.VMEM((hp, 2 * tq, LANES), F32),
            pltpu.VMEM((hp, 2 * tq, LANES), F32),
            pltpu.VMEM((hp, 2 * tq, DIFF_V_DIM + LANES), F32),
            pltpu.VMEM((hp, s_len, DIFF_V_DIM + LANES), BF16),
        ],
        compiler_params=_cparams(("parallel", "parallel")),
        name="dattn",
    )(proj3, proj3, proj3, lam_params, g_subln)


def _sb_terms(q, k, u, strict):
    z = lax.dot_general(q, k, (((1,), (1,)), ((), ())), preferred_element_type=F32)
    l1p = jnp.log2(1.0 + jnp.exp2(-jnp.abs(z)))
    lom = jnp.minimum(-z, 0.0) - l1p
    log_beta = lom + z
    if strict is not None:
        lom = jnp.where(strict, lom, 0.0)
    hi = lom.astype(BF16)
    lo = (lom - hi.astype(F32)).astype(BF16)
    suffix = jnp.dot(hi, u, preferred_element_type=F32) + jnp.dot(lo, u, preferred_element_type=F32)
    return log_beta, lom, suffix


def _lane_rowsum(x):
    return jnp.broadcast_to(jnp.sum(x, axis=-1, keepdims=True), (x.shape[0], LANES))


def _exp_shifted(x, rb):
    nc = x.shape[1] // LANES
    return jnp.concatenate([jnp.exp2(x[:, c * LANES:(c + 1) * LANES] + rb) for c in range(nc)], axis=1)


def _sbattn_kernel(q_ref, k_ref, v_ref, u_ref, o_ref, rb_ref, acc_ref, *, tq):
    s_len = q_ref.shape[1]
    n_heads = q_ref.shape[2] // LANES
    nq = s_len // tq
    row = lax.broadcasted_iota(jnp.int32, (tq, tq), 0)
    col = lax.broadcasted_iota(jnp.int32, (tq, tq), 1)
    strict = col < row
    heads = [(hd, slice(hd * LANES, (hd + 1) * LANES)) for hd in range(n_heads)]

    def rows(i):
        return pl.ds(pl.multiple_of(i * tq, tq), tq)

    def q_tile(qi, first):
        u = u_ref[...]
        rmax = None
        for hd, cols in heads:
            q = q_ref[0, rows(qi), cols]
            log_beta, lom, suffix = _sb_terms(q, k_ref[0, rows(qi), cols], u, strict)
            a = jnp.where(strict, jnp.exp2(log_beta + suffix), 0.0)
            acc = jnp.dot(a.astype(BF16), v_ref[0, rows(qi), cols], preferred_element_type=F32)
            rb = _lane_rowsum(lom)
            if first:
                o_ref[0, rows(qi), cols] = acc.astype(o_ref.dtype)
                continue
            log_beta, lom, suffix = _sb_terms(q, k_ref[0, rows(qi - 1), cols], u, None)
            a = _exp_shifted(log_beta + suffix, rb)
            acc_ref[hd] = acc + jnp.dot(a.astype(BF16), v_ref[0, rows(qi - 1), cols],
                                        preferred_element_type=F32)
            rb = rb + _lane_rowsum(lom)
            rb_ref[hd] = rb
            rmax = jnp.max(rb) if rmax is None else jnp.maximum(rmax, jnp.max(rb))
        if first:
            return

        def cond(c):
            n, rmax = c
            return (n <= qi) & (rmax > SB_CUTOFF)

        def body(c):
            n, _ = c
            kj = qi - n
            rmax = None
            for hd, cols in heads:
                log_beta, lom, suffix = _sb_terms(q_ref[0, rows(qi), cols], k_ref[0, rows(kj), cols],
                                                  u_ref[...], None)
                a = _exp_shifted(log_beta + suffix, rb_ref[hd])
                acc_ref[hd] += jnp.dot(a.astype(BF16), v_ref[0, rows(kj), cols], preferred_element_type=F32)
                rb_new = rb_ref[hd] + _lane_rowsum(lom)
                rb_ref[hd] = rb_new
                rmax = jnp.max(rb_new) if rmax is None else jnp.maximum(rmax, jnp.max(rb_new))
            return n + 1, rmax

        lax.while_loop(cond, body, (jnp.int32(2), rmax))
        for hd, cols in heads:
            o_ref[0, rows(qi), cols] = acc_ref[hd].astype(o_ref.dtype)

    q_tile(0, True)

    def q_body(qi, carry):
        q_tile(qi, False)
        return carry

    lax.fori_loop(1, nq, q_body, 0)


def _sbattn(proj3, tri, *, tq, col0, heads_per_step):
    b, s_len, _ = proj3.shape
    h = N_HEADS_SB
    hp = heads_per_step
    kern = functools.partial(_sbattn_kernel, tq=tq)
    blk = lambda off: pl.BlockSpec((1, s_len, hp * LANES), lambda bi, hi: (bi, 0, off // hp + hi))
    return pl.pallas_call(
        kern,
        grid=(b, h // hp),
        in_specs=[
            blk(col0), blk(col0 + h), blk(col0 + 2 * h),
            pl.BlockSpec((tq, tq), lambda bi, hi: (0, 0)),
        ],
        out_specs=pl.BlockSpec((1, s_len, hp * LANES), lambda bi, hi: (bi, 0, hi)),
        out_shape=jax.ShapeDtypeStruct((b, s_len, h * SB_HEAD_DIM), BF16),
        scratch_shapes=[
            pltpu.VMEM((hp, tq, LANES), F32),
            pltpu.VMEM((hp, tq, SB_HEAD_DIM), F32),
        ],
        compiler_params=_cparams(("parallel", "parallel")),
        name="sbattn",
    )(proj3, proj3, proj3, tri)


def _split_bf16(a):
    hi = a.astype(BF16)
    return hi, (a - hi.astype(F32)).astype(BF16)


def _mix_kernel(od_ref, os_ref, g0_ref, g1_ref, x_ref, wd_ref, ws_ref, wo_ref, gm_ref,
                wr_hi_ref, wr_lo_ref, br_ref, ltri_ref, x1_ref, hn_ref, route_ref, cnt_ref):
    @pl.when(pl.program_id(0) == 0)
    def _():
        cnt_ref[...] = jnp.zeros(cnt_ref.shape, F32)

    yd = jnp.dot(od_ref[...], wd_ref[...], preferred_element_type=F32)
    ys = jnp.dot(os_ref[...], ws_ref[...], preferred_element_type=F32)
    mixed = g0_ref[...].astype(F32) * yd + g1_ref[...].astype(F32) * ys
    x1 = x_ref[...] + jnp.dot(mixed.astype(BF16), wo_ref[...], preferred_element_type=F32)
    x1_ref[...] = x1
    hn = _rms(x1, gm_ref[...])
    hn_hi, hn_lo = _split_bf16(hn)
    tm = hn.shape[0]
    hn_f = hn_hi.astype(F32)
    for c in range(SLAB_PITCH):
        part = hn_f[:, c * LANES:(c + 1) * LANES] if c < SLAB_ROWS else jnp.zeros((tm, LANES), F32)
        hn_ref[pl.ds(c, tm, stride=SLAB_PITCH), :] = part

    quad = jnp.dot(jnp.concatenate([hn_hi, hn_lo], axis=0),
                   jnp.concatenate([wr_hi_ref[...], wr_lo_ref[...]], axis=1), preferred_element_type=F32)
    logits = (quad[:tm, :LANES] + quad[:tm, LANES:]) + (quad[tm:, :LANES] + quad[tm:, LANES:]) + br_ref[...]
    lane = lax.broadcasted_iota(jnp.int32, logits.shape, 1)
    big = jnp.int32(LANES)

    def first_argmax(vals, vmax):
        return jnp.min(jnp.where(vals == vmax, lane, big), axis=-1, keepdims=True)

    is_group = (lane >= N_EXPERTS) & (lane < N_EXPERTS + N_GROUPS)
    gl = jnp.where(is_group, logits, -jnp.inf)
    gmax = jnp.max(gl, axis=-1, keepdims=True)
    gexp = jnp.exp(gl - gmax)
    probs = gexp / jnp.sum(gexp, axis=-1, keepdims=True)
    g_val = jnp.max(probs, axis=-1, keepdims=True)
    g_idx = first_argmax(jnp.where(is_group, probs, -jnp.inf), g_val) - N_EXPERTS

    in_group = (lane >= g_idx * EXPERTS_PER_GROUP) & (lane < (g_idx + 1) * EXPERTS_PER_GROUP)
    el = jnp.where(in_group, logits, -jnp.inf)
    v1 = jnp.max(el, axis=-1, keepdims=True)
    i1 = first_argmax(el, v1)
    el2 = jnp.where(lane == i1, -jnp.inf, el)
    v2 = jnp.max(el2, axis=-1, keepdims=True)
    i2 = first_argmax(el2, v2)
    e2 = jnp.exp(v2 - v1)
    den = 1.0 + e2
    w1 = (1.0 / den) * g_val
    w2 = (e2 / den) * g_val

    pick1 = lane == i1
    pick2 = lane == i2
    onehot = jnp.where(pick1 | pick2, 1.0, 0.0).astype(BF16)
    seen = jnp.dot(ltri_ref[...], onehot, preferred_element_type=F32) + cnt_ref[...]
    rank1 = jnp.sum(jnp.where(pick1, seen, 0.0), axis=-1, keepdims=True) - 1.0
    rank2 = jnp.sum(jnp.where(pick2, seen, 0.0), axis=-1, keepdims=True) - 1.0
    cnt_ref[...] = seen[seen.shape[0] - 1:, :]
    cols = (w1, w2, i1.astype(F32), i2.astype(F32), rank1, rank2)
    route = jnp.zeros(logits.shape, F32)
    for c, val in enumerate(cols):
        route = jnp.where(lane == c, val, route)
    route_ref[...] = route


def _mix(o_diff, o_sb, proj, x2, w_brd, w_brs, w_out, g_moe, wr_hi, wr_lo, b_r, *, tm, gate_col0):
    t_tok, d = x2.shape
    wd_in = o_diff.shape[1]
    ws_in = o_sb.shape[1]
    gblk = gate_col0 // d
    r = np.arange(tm)
    ltri = jnp.asarray(r[None, :] <= r[:, None], dtype=BF16)
    const = lambda shape: pl.BlockSpec(shape, lambda i: (0, 0), pipeline_mode=pl.Buffered(1))
    return pl.pallas_call(
        _mix_kernel,
        grid=(t_tok // tm,),
        in_specs=[
            pl.BlockSpec((tm, wd_in), lambda i: (i, 0)),
            pl.BlockSpec((tm, ws_in), lambda i: (i, 0)),
            pl.BlockSpec((tm, d), lambda i: (i, gblk)),
            pl.BlockSpec((tm, d), lambda i: (i, gblk + 1)),
            pl.BlockSpec((tm, d), lambda i: (i, 0)),
            const(w_brd.shape), const(w_brs.shape), const(w_out.shape),
            const(g_moe.shape), const(wr_hi.shape), const(wr_lo.shape), const(b_r.shape),
            const(ltri.shape),
        ],
        out_specs=[
            pl.BlockSpec((tm, d), lambda i: (i, 0)),
            pl.BlockSpec((tm * SLAB_PITCH, LANES), lambda i: (i, 0)),
            pl.BlockSpec((tm, LANES), lambda i: (i, 0)),
            pl.BlockSpec((1, LANES), lambda i: (0, 0)),
        ],
        out_shape=[
            jax.ShapeDtypeStruct((t_tok, d), F32),
            jax.ShapeDtypeStruct((t_tok * SLAB_PITCH, LANES), F32),
            jax.ShapeDtypeStruct((t_tok, LANES), F32),
            jax.ShapeDtypeStruct((1, LANES), F32),
        ],
        compiler_params=_cparams(("arbitrary",)),
        name="mix",
    )(o_diff, o_sb, proj, proj, x2, w_brd, w_brs, w_out, g_moe, wr_hi, wr_lo, b_r, ltri)


def _gather_rows(src_hbm, idx_ref, base, dst, sem, n_rows):
    for r in range(n_rows):
        row = idx_ref[base + r]
        pltpu.make_async_copy(src_hbm.at[pl.ds(row, 1), :], dst.at[pl.ds(r, 1), :], sem).start(priority=r % 2)


def _wait_rows(src_hbm, dst, sem):
    pltpu.make_async_copy(src_hbm.at[pl.ds(0, dst.shape[0]), :], dst, sem).wait()


def _gather_slabs(src_hbm, idx_ref, base, dst, sem, n_items):
    for r in range(n_items):
        tok = idx_ref[base + r]
        pltpu.make_async_copy(src_hbm.at[pl.ds(tok * SLAB_PITCH, SLAB_ROWS), :],
                              dst.at[pl.ds(r * SLAB_PITCH, SLAB_ROWS), :], sem).start(priority=r % 2)


def _wait_slabs(src_hbm, dst, sem, n_items):
    n = n_items * SLAB_ROWS
    pltpu.make_async_copy(src_hbm.at[pl.ds(0, n), :], dst.at[pl.ds(0, n), :], sem).wait()


def _experts_kernel(te_ref, nused_ref, src_ref, hn_hbm, wg_ref, wu_ref, wd_ref, y_ref, xbuf, sem):
    i = pl.program_id(0)
    tm = y_ref.shape[0]
    n_used = nused_ref[0]
    slot = i % 2

    @pl.when(i == 0)
    def _():
        _gather_slabs(hn_hbm, src_ref, 0, xbuf.at[0], sem.at[0], tm)

    @pl.when(i + 1 < n_used)
    def _():
        _gather_slabs(hn_hbm, src_ref, (i + 1) * tm, xbuf.at[1 - slot], sem.at[1 - slot], tm)

    @pl.when(i < n_used)
    def _():
        _wait_slabs(hn_hbm, xbuf.at[slot], sem.at[slot], tm)
        x = jnp.concatenate([xbuf[slot, pl.ds(c, tm, stride=SLAB_PITCH), :] for c in range(SLAB_ROWS)],
                            axis=1).astype(BF16)
        g = jnp.dot(x, wg_ref[0], preferred_element_type=F32)
        u = jnp.dot(x, wu_ref[0], preferred_element_type=F32)
        a = (jax.nn.silu(g) * u).astype(BF16)
        y_ref[...] = jnp.dot(a, wd_ref[0], preferred_element_type=F32)

    @pl.when(i >= n_used)
    def _():
        y_ref[...] = jnp.zeros(y_ref.shape, y_ref.dtype)


def _experts(tile_expert, n_used, src, hn_slabs, w_g, w_u, w_d, *, tm):
    n_rows = src.shape[0]
    n_e, d, ff = w_g.shape
    assert d == SLAB_ROWS * LANES
    grid_spec = pltpu.PrefetchScalarGridSpec(
        num_scalar_prefetch=3,
        grid=(n_rows // tm,),
        in_specs=[
            pl.BlockSpec(memory_space=pl.ANY),
            pl.BlockSpec((1, d, ff), lambda i, te, nu, sr: (te[i], 0, 0)),
            pl.BlockSpec((1, d, ff), lambda i, te, nu, sr: (te[i], 0, 0)),
            pl.BlockSpec((1, ff, d), lambda i, te, nu, sr: (te[i], 0, 0)),
        ],
        out_specs=pl.BlockSpec((tm, d), lambda i, te, nu, sr: (i, 0)),
        scratch_shapes=[pltpu.VMEM((2, tm * SLAB_PITCH, LANES), F32), pltpu.SemaphoreType.DMA((2,))],
    )
    return pl.pallas_call(
        _experts_kernel,
        grid_spec=grid_spec,
        out_shape=jax.ShapeDtypeStruct((n_rows, d), F32),
        compiler_params=pltpu.CompilerParams(dimension_semantics=("arbitrary",),
                                             vmem_limit_bytes=VMEM_LIMIT, disable_bounds_checks=True),
        name="experts",
    )(tile_expert, n_used, src, hn_slabs, w_g, w_u, w_d)


def _combine_kernel(d1_ref, d2_ref, y_hbm, x1_ref, route_ref, gf_ref, o_ref, ybuf, sem):
    i = pl.program_id(0)
    n = pl.num_programs(0)
    tm = x1_ref.shape[0]
    slot = i % 2

    def gather(tile, s):
        _gather_rows(y_hbm, d1_ref, tile * tm, ybuf.at[s, 0], sem.at[s], tm)
        _gather_rows(y_hbm, d2_ref, tile * tm, ybuf.at[s, 1], sem.at[s], tm)

    @pl.when(i == 0)
    def _():
        gather(0, 0)

    @pl.when(i + 1 < n)
    def _():
        gather(i + 1, 1 - slot)

    _wait_rows(y_hbm, ybuf.at[slot, 0], sem.at[slot])
    _wait_rows(y_hbm, ybuf.at[slot, 1], sem.at[slot])
    route = route_ref[...]
    x2 = x1_ref[...] + route[:, 0:1] * ybuf[slot, 0] + route[:, 1:2] * ybuf[slot, 1]
    o_ref[...] = _rms(x2, gf_ref[...])


def _combine(d1, d2, y, x1, route, g_final, *, tm):
    t_tok, d = x1.shape
    grid_spec = pltpu.PrefetchScalarGridSpec(
        num_scalar_prefetch=2,
        grid=(t_tok // tm,),
        in_specs=[
            pl.BlockSpec(memory_space=pl.ANY),
            pl.BlockSpec((tm, d), lambda i, a, b: (i, 0)),
            pl.BlockSpec((tm, LANES), lambda i, a, b: (i, 0)),
            pl.BlockSpec((1, d), lambda i, a, b: (0, 0)),
        ],
        out_specs=pl.BlockSpec((tm, d), lambda i, a, b: (i, 0)),
        scratch_shapes=[pltpu.VMEM((2, 2, tm, d), F32), pltpu.SemaphoreType.DMA((2,))],
    )
    return pl.pallas_call(
        _combine_kernel,
        grid_spec=grid_spec,
        out_shape=jax.ShapeDtypeStruct((t_tok, d), F32),
        compiler_params=pltpu.CompilerParams(dimension_semantics=("arbitrary",),
                                             vmem_limit_bytes=VMEM_LIMIT, disable_bounds_checks=True),
        name="combine",
    )(d1, d2, y, x1, route, g_final)


def _pair_rows_kernel(route_ref, off_ref, d_ref):
    route = route_ref[...]
    off = off_ref[...]
    lane = lax.broadcasted_iota(jnp.int32, route.shape, 1)
    lane_f = lane.astype(F32)

    def row_of(e_col, rank_col):
        first = jnp.sum(jnp.where(lane_f == route[:, e_col:e_col + 1], off, 0.0), axis=-1, keepdims=True)
        return first + route[:, rank_col:rank_col + 1]

    d = jnp.where(lane == 0, row_of(2, 4), jnp.where(lane == 1, row_of(3, 5), 0.0))
    d_ref[...] = d.astype(jnp.int32)


def _pair_rows(route, off_row, *, tm):
    t_tok = route.shape[0]
    return pl.pallas_call(
        _pair_rows_kernel,
        grid=(t_tok // tm,),
        in_specs=[pl.BlockSpec((tm, LANES), lambda i: (i, 0)), pl.BlockSpec((1, LANES), lambda i: (0, 0))],
        out_specs=pl.BlockSpec((tm, LANES), lambda i: (i, 0)),
        out_shape=jax.ShapeDtypeStruct((t_tok, LANES), jnp.int32),
        compiler_params=_cparams(("parallel",)),
        name="pair_rows",
    )(route, off_row)


def _dispatch_plan(route, counts, *, tm, n_tiles):
    t_tok = route.shape[0]
    cnt = counts[0, :N_EXPERTS].astype(jnp.int32)
    tiles_per_e = (cnt + tm - 1) // tm
    tile_end = jnp.cumsum(tiles_per_e)
    off = (tile_end - tiles_per_e) * tm
    off_row = jnp.pad(off.astype(F32), (0, LANES - N_EXPERTS)).reshape(1, LANES)
    pair_rows = _pair_rows(route, off_row, tm=_tile(t_tok, 2048))
    d1 = pair_rows[:, 0]
    d2 = pair_rows[:, 1]
    tok = jnp.arange(t_tok, dtype=jnp.int32)
    src = jnp.zeros((n_tiles * tm,), jnp.int32).at[jnp.concatenate([d1, d2])].add(
        jnp.concatenate([tok, tok]))
    tile_ids = jnp.arange(n_tiles, dtype=jnp.int32)
    tile_expert = jnp.minimum(
        jnp.sum((tile_end[None, :] <= tile_ids[:, None]).astype(jnp.int32), axis=1), N_EXPERTS - 1)
    return d1, d2, src, tile_expert, tile_end[-1:].astype(jnp.int32)


def _tile(n, pref):
    t = min(n, pref)
    assert n % t == 0, (n, t)
    return t


def kernel(x, positions, g_mix, w_in, w_gate, lambda_params, g_subln, w_br_diff, w_br_sb, w_out,
           g_moe, w_group, b_group, w_router, b_router, w_up_gate, w_up, w_down, g_final):
    b, s_len, d = x.shape
    t_tok = b * s_len
    depth = w_in.shape[0]
    assert depth == 1
    l = 0
    x2 = x.reshape(t_tok, d)

    inv_freq = ROPE_THETA ** (-jnp.arange(0, ROT_DIM, 2, dtype=F32) / ROT_DIM)
    lane = np.arange(LANES)
    invf = inv_freq[(lane % DIFF_HEAD_DIM) % (ROT_DIM // 2)].reshape(1, LANES)
    posf = positions.astype(F32).reshape(t_tok, 1)

    in_w = w_in.shape[2]
    w_cat = jnp.concatenate([w_in[l].astype(BF16), w_gate[l].astype(BF16)], axis=1)
    tn = 2048
    qk_w = N_HEADS_DIFF * 2 * DIFF_HEAD_DIM
    sbq_col0 = 2 * qk_w + N_HEADS_DIFF * DIFF_V_DIM
    assert (2 * qk_w) % tn == 0 and (in_w - 2 * qk_w) % tn == 0
    proj = _proj(x2, posf, invf, g_mix[l].reshape(1, d), w_cat,
                 tm=_tile(t_tok, 1024), tn=tn,
                 n_rope_tiles=2 * qk_w // tn, n_plain_tiles=(in_w - 2 * qk_w) // tn,
                 dq_cols=(0, qk_w), sbq_cols=(sbq_col0, sbq_col0 + N_HEADS_SB * SB_HEAD_DIM))
    proj3 = proj.reshape(b, s_len, proj.shape[1])

    o_diff = _dattn(proj3, lambda_params[l], g_subln[l].reshape(1, DIFF_V_DIM), tq=_tile(s_len, 512),
                    heads_per_step=2)
    tq = _tile(s_len, 256)
    r = np.arange(tq)
    tri = jnp.asarray(r[:, None] > r[None, :], dtype=BF16)
    sb_col0 = (2 * qk_w + N_HEADS_DIFF * DIFF_V_DIM) // LANES
    o_sb = _sbattn(proj3, tri, tq=tq, col0=sb_col0, heads_per_step=8)

    w_r = jnp.concatenate([w_router[l], w_group[l]], axis=1)
    w_r = jnp.pad(w_r, ((0, 0), (0, LANES - w_r.shape[1])))
    wr_hi = w_r.astype(BF16)
    wr_lo = (w_r - wr_hi.astype(F32)).astype(BF16)
    b_r = jnp.pad(jnp.concatenate([b_router[l], b_group[l]]), (0, LANES - N_EXPERTS - N_GROUPS))
    x1, hn, route, counts = _mix(
        o_diff.reshape(t_tok, -1), o_sb.reshape(t_tok, -1), proj, x2,
        w_br_diff[l].astype(BF16), w_br_sb[l].astype(BF16), w_out[l].astype(BF16),
        g_moe[l].reshape(1, d), wr_hi, wr_lo, b_r.reshape(1, LANES),
        tm=_tile(t_tok, 256), gate_col0=in_w)

    tm_e = _tile(t_tok, 512)
    n_tiles = 2 * t_tok // tm_e + N_EXPERTS
    d1, d2, src, tile_expert, n_used = _dispatch_plan(route, counts, tm=tm_e, n_tiles=n_tiles)
    y = _experts(tile_expert, n_used, src, hn,
                 w_up_gate[l].astype(BF16), w_up[l].astype(BF16), w_down[l].astype(BF16), tm=tm_e)
    out = _combine(d1, d2, y, x1, route, g_final.reshape(1, d), tm=_tile(t_tok, 512))
    return out.reshape(b, s_len, d)
```

```python
import functools

import numpy as np
import jax
import jax.numpy as jnp
from jax import lax
from jax.experimental import pallas as pl
from jax.experimental.pallas import tpu as pltpu

F32 = jnp.float32
BF16 = jnp.bfloat16

N_HEADS_DIFF = 8
DIFF_HEAD_DIM = 64
DIFF_V_DIM = 128
N_HEADS_SB = 8
SB_HEAD_DIM = 128
ROT_DIM = 16
ROPE_THETA = 500000.0
N_GROUPS = 4
EXPERTS_PER_GROUP = 8
N_EXPERTS = 32
EPS = 1e-6
NEG_INF = float(np.finfo(np.float32).min)
SB_CUTOFF = -150.5
LAMBDA_INIT = 0.8 - 0.6 * float(np.exp(-0.3 * 0))
LOG2E = float(np.log2(np.e))

LANES = 128
MXU_N = 256
SLAB_ROWS = 16
SLAB_PITCH = 20
VMEM_LIMIT = 56 * 1024 * 1024


def _cparams(sem):
    return pltpu.CompilerParams(dimension_semantics=sem, vmem_limit_bytes=VMEM_LIMIT)


def _rms(x, g):
    return x * lax.rsqrt(jnp.mean(x * x, axis=-1, keepdims=True) + EPS) * g


def _proj_kernel(x_ref, pos_ref, invf_ref, g_ref, w_ref, o_ref, h_ref, c_ref, s1_ref, s2_ref,
                 *, n_rope_tiles, n_plain_tiles, dq_cols, sbq_cols):
    j = pl.program_id(1)
    tn = o_ref.shape[1]

    @pl.when(j == 0)
    def _():
        h_ref[...] = _rms(x_ref[...], g_ref[...]).astype(BF16)
        ang = pos_ref[...] * invf_ref[...]
        lane = lax.broadcasted_iota(jnp.int32, ang.shape, 1) % DIFF_HEAD_DIM
        cos = jnp.cos(ang)
        sin = jnp.sin(ang)
        half = ROT_DIM // 2
        c_ref[...] = jnp.where(lane < ROT_DIM, cos, 1.0)
        s1_ref[...] = jnp.where(lane < half, -sin, 0.0)
        s2_ref[...] = jnp.where((lane >= half) & (lane < ROT_DIM), sin, 0.0)

    def chunks():
        h = h_ref[...]
        for c in range(tn // MXU_N):
            cols = slice(c * MXU_N, (c + 1) * MXU_N)
            yield cols, j * tn + c * MXU_N, jnp.dot(h, w_ref[:, cols], preferred_element_type=F32)

    def scale_in(col0, col_range, value):
        return jnp.where((col0 >= col_range[0]) & (col0 < col_range[1]), value, 1.0).astype(F32)

    @pl.when(j < n_rope_tiles)
    def _():
        half = ROT_DIM // 2
        for cols, col0, acc in chunks():
            qscale = scale_in(col0, dq_cols, DIFF_HEAD_DIM ** -0.5 * LOG2E)
            parts = []
            for c in range(MXU_N // LANES):
                t = acc[:, c * LANES:(c + 1) * LANES]
                up = pltpu.roll(t, LANES - half, axis=1)
                dn = pltpu.roll(t, half, axis=1)
                parts.append((t * c_ref[...] + up * s1_ref[...] + dn * s2_ref[...]) * qscale)
            o_ref[:, cols] = jnp.concatenate(parts, axis=1).astype(o_ref.dtype)

    @pl.when((j >= n_rope_tiles) & (j < n_rope_tiles + n_plain_tiles))
    def _():
        for cols, col0, acc in chunks():
            scale = scale_in(col0, sbq_cols, SB_HEAD_DIM ** -0.5 * LOG2E)
            o_ref[:, cols] = (acc * scale).astype(o_ref.dtype)

    @pl.when(j >= n_rope_tiles + n_plain_tiles)
    def _():
        for cols, _, acc in chunks():
            o_ref[:, cols] = jax.nn.sigmoid(acc).astype(o_ref.dtype)


def _proj(x2, posf, invf, g_mix, w_cat, *, tm, tn, n_rope_tiles, n_plain_tiles, dq_cols, sbq_cols):
    t_tok, d = x2.shape
    n = w_cat.shape[1]
    kern = functools.partial(_proj_kernel, n_rope_tiles=n_rope_tiles, n_plain_tiles=n_plain_tiles,
                             dq_cols=dq_cols, sbq_cols=sbq_cols)
    return pl.pallas_call(
        kern,
        grid=(t_tok // tm, n // tn),
        in_specs=[
            pl.BlockSpec((tm, d), lambda i, j: (i, 0)),
            pl.BlockSpec((tm, 1), lambda i, j: (i, 0)),
            pl.BlockSpec((1, LANES), lambda i, j: (0, 0)),
            pl.BlockSpec((1, d), lambda i, j: (0, 0)),
            pl.BlockSpec((d, tn), lambda i, j: (0, j)),
        ],
        out_specs=pl.BlockSpec((tm, tn), lambda i, j: (i, j)),
        out_shape=jax.ShapeDtypeStruct((t_tok, n), BF16),
        scratch_shapes=[
            pltpu.VMEM((tm, d), BF16),
            pltpu.VMEM((tm, LANES), F32),
            pltpu.VMEM((tm, LANES), F32),
            pltpu.VMEM((tm, LANES), F32),
        ],
        compiler_params=_cparams(("parallel", "arbitrary")),
        name="proj",
    )(x2, posf, invf, g_mix, w_cat)


def _dattn_kernel(q_ref, k_ref, v_ref, lp_ref, g_ref, o_ref, s_ref, ml_ref, mb_ref, acc_ref, vx_ref, *, tq):
    s_len = q_ref.shape[1]
    n_heads = q_ref.shape[2] // LANES
    nq = s_len // tq
    nc = tq // LANES
    lp = lp_ref[...]
    lam = (jnp.exp(jnp.sum(lp[0:1] * lp[1:2], axis=-1, keepdims=True))
           - jnp.exp(jnp.sum(lp[2:3] * lp[3:4], axis=-1, keepdims=True)) + LAMBDA_INIT)
    lane = lax.broadcasted_iota(jnp.int32, (tq, LANES), 1)
    row = lax.broadcasted_iota(jnp.int32, (2 * tq, tq), 0) % tq
    col = lax.broadcasted_iota(jnp.int32, (2 * tq, tq), 1)
    causal = col <= row
    heads = [(hd, slice(hd * LANES, (hd + 1) * LANES)) for hd in range(n_heads)]
    for hd, cols in heads:
        vx_ref[hd, :, :DIFF_V_DIM] = v_ref[0, :, cols]
        vx_ref[hd, :, DIFF_V_DIM:] = jnp.ones((s_len, LANES), vx_ref.dtype)

    def rows(i):
        return pl.ds(pl.multiple_of(i * tq, tq), tq)

    def q_body(qi, carry):
        qqs = []
        for hd, cols in heads:
            q = q_ref[0, rows(qi), cols]
            zero = jnp.zeros_like(q)
            qqs.append(jnp.concatenate([jnp.where(lane < DIFF_HEAD_DIM, q, zero),
                                        jnp.where(lane >= DIFF_HEAD_DIM, q, zero)], axis=0))
        ml_ref[...] = jnp.full(ml_ref.shape, NEG_INF, F32)

        def scores(kj, masked):
            for hd, cols in heads:
                k = k_ref[0, rows(kj), cols]
                s = lax.dot_general(qqs[hd], k, (((1,), (1,)), ((), ())), preferred_element_type=F32)
                if masked:
                    s = jnp.where(causal, s, NEG_INF)
                s_ref[hd, kj] = s
                m = ml_ref[hd]
                for c in range(nc):
                    m = jnp.maximum(m, s[:, c * LANES:(c + 1) * LANES])
                ml_ref[hd] = m

        def s_body(kj, c):
            scores(kj, False)
            return c

        lax.fori_loop(0, qi, s_body, 0)
        scores(qi, True)

        for hd, _ in heads:
            mb_ref[hd] = jnp.broadcast_to(jnp.max(ml_ref[hd], axis=-1, keepdims=True), mb_ref.shape[1:])
        acc_ref[...] = jnp.zeros(acc_ref.shape, F32)

        def av_body(kj, c):
            for hd, _ in heads:
                mb = mb_ref[hd]
                p = jnp.concatenate(
                    [jnp.exp2(s_ref[hd, kj, :, c * LANES:(c + 1) * LANES] - mb) for c in range(nc)], axis=1)
                acc_ref[hd] += jnp.dot(p.astype(BF16), vx_ref[hd, rows(kj), :], preferred_element_type=F32)
            return c

        lax.fori_loop(0, qi + 1, av_body, 0)

        for hd, cols in heads:
            o = acc_ref[hd, :, :DIFF_V_DIM] / acc_ref[hd, :, DIFF_V_DIM:]
            od = o[:tq] - lam * o[tq:]
            od = _rms(od, g_ref[...]) * (1.0 - LAMBDA_INIT)
            o_ref[0, rows(qi), cols] = od.astype(o_ref.dtype)
        return carry

    lax.fori_loop(0, nq, q_body, 0)


def _dattn(proj3, lam_params, g_subln, *, tq, heads_per_step):
    b, s_len, _ = proj3.shape
    h = N_HEADS_DIFF
    hp = heads_per_step
    kern = functools.partial(_dattn_kernel, tq=tq)
    blk = lambda off: pl.BlockSpec((1, s_len, hp * LANES), lambda bi, hi: (bi, 0, off // hp + hi))
    return pl.pallas_call(
        kern,
        grid=(b, h // hp),
        in_specs=[
            blk(0), blk(h), blk(2 * h),
            pl.BlockSpec(lam_params.shape, lambda bi, hi: (0, 0)),
            pl.BlockSpec((1, DIFF_V_DIM), lambda bi, hi: (0, 0)),
        ],
        out_specs=pl.BlockSpec((1, s_len, hp * LANES), lambda bi, hi: (bi, 0, hi)),
        out_shape=jax.ShapeDtypeStruct((b, s_len, h * DIFF_V_DIM), BF16),
        scratch_shapes=[
            pltpu.VMEM((hp, s_len // tq, 2 * tq, tq), F32),
            pltpu.VMEM((hp, 2 * tq, LANES), F32),
            pltpu.VMEM((hp, 2 * tq, LANES), F32),
            pltpu.VMEM((hp, 2 * tq, DIFF_V_DIM + LANES), F32),
            pltpu.VMEM((hp, s_len, DIFF_V_DIM + LANES), BF16),
        ],
        compiler_params=_cparams(("parallel", "parallel")),
        name="dattn",
    )(proj3, proj3, proj3, lam_params, g_subln)


def _sb_terms(q, k, u, strict):
    z = lax.dot_general(q, k, (((1,), (1,)), ((), ())), preferred_element_type=F32)
    l1p = jnp.log2(1.0 + jnp.exp2(-jnp.abs(z)))
    lom = jnp.minimum(-z, 0.0) - l1p
    log_beta = lom + z
    if strict is not None:
        lom = jnp.where(strict, lom, 0.0)
    hi = lom.astype(BF16)
    lo = (lom - hi.astype(F32)).astype(BF16)
    suffix = jnp.dot(hi, u, preferred_element_type=F32) + jnp.dot(lo, u, preferred_element_type=F32)
    return log_beta, lom, suffix


def _lane_rowsum(x):
    return jnp.broadcast_to(jnp.sum(x, axis=-1, keepdims=True), (x.shape[0], LANES))


def _exp_shifted(x, rb):
    nc = x.shape[1] // LANES
    return jnp.concatenate([jnp.exp2(x[:, c * LANES:(c + 1) * LANES] + rb) for c in range(nc)], axis=1)


def _sbattn_kernel(q_ref, k_ref, v_ref, u_ref, o_ref, rb_ref, acc_ref, *, tq):
    s_len = q_ref.shape[1]
    n_heads = q_ref.shape[2] // LANES
    nq = s_len // tq
    row = lax.broadcasted_iota(jnp.int32, (tq, tq), 0)
    col = lax.broadcasted_iota(jnp.int32, (tq, tq), 1)
    strict = col < row
    heads = [(hd, slice(hd * LANES, (hd + 1) * LANES)) for hd in range(n_heads)]

    def rows(i):
        return pl.ds(pl.multiple_of(i * tq, tq), tq)

    def q_tile(qi, first):
        u = u_ref[...]
        rmax = None
        for hd, cols in heads:
            q = q_ref[0, rows(qi), cols]
            log_beta, lom, suffix = _sb_terms(q, k_ref[0, rows(qi), cols], u, strict)
            a = jnp.where(strict, jnp.exp2(log_beta + suffix), 0.0)
            acc = jnp.dot(a.astype(BF16), v_ref[0, rows(qi), cols], preferred_element_type=F32)
            rb = _lane_rowsum(lom)
            if first:
                o_ref[0, rows(qi), cols] = acc.astype(o_ref.dtype)
                continue
            log_beta, lom, suffix = _sb_terms(q, k_ref[0, rows(qi - 1), cols], u, None)
            a = _exp_shifted(log_beta + suffix, rb)
            acc_ref[hd] = acc + jnp.dot(a.astype(BF16), v_ref[0, rows(qi - 1), cols],
                                        preferred_element_type=F32)
            rb = rb + _lane_rowsum(lom)
            rb_ref[hd] = rb
            rmax = jnp.max(rb) if rmax is None else jnp.maximum(rmax, jnp.max(rb))
        if first:
            return

        def cond(c):
            n, rmax = c
            return (n <= qi) & (rmax > SB_CUTOFF)

        def body(c):
            n, _ = c
            kj = qi - n
            rmax = None
            for hd, cols in heads:
                log_beta, lom, suffix = _sb_terms(q_ref[0, rows(qi), cols], k_ref[0, rows(kj), cols],
                                                  u_ref[...], None)
                a = _exp_shifted(log_beta + suffix, rb_ref[hd])
                acc_ref[hd] += jnp.dot(a.astype(BF16), v_ref[0, rows(kj), cols], preferred_element_type=F32)
                rb_new = rb_ref[hd] + _lane_rowsum(lom)
                rb_ref[hd] = rb_new
                rmax = jnp.max(rb_new) if rmax is None else jnp.maximum(rmax, jnp.max(rb_new))
            return n + 1, rmax

        lax.while_loop(cond, body, (jnp.int32(2), rmax))
        for hd, cols in heads:
            o_ref[0, rows(qi), cols] = acc_ref[hd].astype(o_ref.dtype)

    q_tile(0, True)

    def q_body(qi, carry):
        q_tile(qi, False)
        return carry

    lax.fori_loop(1, nq, q_body, 0)


def _sbattn(proj3, tri, *, tq, col0, heads_per_step):
    b, s_len, _ = proj3.shape
    h = N_HEADS_SB
    hp = heads_per_step
    kern = functools.partial(_sbattn_kernel, tq=tq)
    blk = lambda off: pl.BlockSpec((1, s_len, hp * LANES), lambda bi, hi: (bi, 0, off // hp + hi))
    return pl.pallas_call(
        kern,
        grid=(b, h // hp),
        in_specs=[
            blk(col0), blk(col0 + h), blk(col0 + 2 * h),
            pl.BlockSpec((tq, tq), lambda bi, hi: (0, 0)),
        ],
        out_specs=pl.BlockSpec((1, s_len, hp * LANES), lambda bi, hi: (bi, 0, hi)),
        out_shape=jax.ShapeDtypeStruct((b, s_len, h * SB_HEAD_DIM), BF16),
        scratch_shapes=[
            pltpu.VMEM((hp, tq, LANES), F32),
            pltpu.VMEM((hp, tq, SB_HEAD_DIM), F32),
        ],
        compiler_params=_cparams(("parallel", "parallel")),
        name="sbattn",
    )(proj3, proj3, proj3, tri)


def _split_bf16(a):
    hi = a.astype(BF16)
    return hi, (a - hi.astype(F32)).astype(BF16)


def _mix_kernel(od_ref, os_ref, g0_ref, g1_ref, x_ref, wd_ref, ws_ref, wo_ref, gm_ref,
                wr_hi_ref, wr_lo_ref, br_ref, ltri_ref, x1_ref, hn_ref, route_ref, cnt_ref):
    @pl.when(pl.program_id(0) == 0)
    def _():
        cnt_ref[...] = jnp.zeros(cnt_ref.shape, F32)

    yd = jnp.dot(od_ref[...], wd_ref[...], preferred_element_type=F32)
    ys = jnp.dot(os_ref[...], ws_ref[...], preferred_element_type=F32)
    mixed = g0_ref[...].astype(F32) * yd + g1_ref[...].astype(F32) * ys
    x1 = x_ref[...] + jnp.dot(mixed.astype(BF16), wo_ref[...], preferred_element_type=F32)
    x1_ref[...] = x1
    hn = _rms(x1, gm_ref[...])
    hn_hi, hn_lo = _split_bf16(hn)
    tm = hn.shape[0]
    hn_f = hn_hi.astype(F32)
    for c in range(SLAB_PITCH):
        part = hn_f[:, c * LANES:(c + 1) * LANES] if c < SLAB_ROWS else jnp.zeros((tm, LANES), F32)
        hn_ref[pl.ds(c, tm, stride=SLAB_PITCH), :] = part

    quad = jnp.dot(jnp.concatenate([hn_hi, hn_lo], axis=0),
                   jnp.concatenate([wr_hi_ref[...], wr_lo_ref[...]], axis=1), preferred_element_type=F32)
    logits = (quad[:tm, :LANES] + quad[:tm, LANES:]) + (quad[tm:, :LANES] + quad[tm:, LANES:]) + br_ref[...]
    lane = lax.broadcasted_iota(jnp.int32, logits.shape, 1)
    big = jnp.int32(LANES)

    def first_argmax(vals, vmax):
        return jnp.min(jnp.where(vals == vmax, lane, big), axis=-1, keepdims=True)

    is_group = (lane >= N_EXPERTS) & (lane < N_EXPERTS + N_GROUPS)
    gl = jnp.where(is_group, logits, -jnp.inf)
    gmax = jnp.max(gl, axis=-1, keepdims=True)
    gexp = jnp.exp(gl - gmax)
    probs = gexp / jnp.sum(gexp, axis=-1, keepdims=True)
    g_val = jnp.max(probs, axis=-1, keepdims=True)
    g_idx = first_argmax(jnp.where(is_group, probs, -jnp.inf), g_val) - N_EXPERTS

    in_group = (lane >= g_idx * EXPERTS_PER_GROUP) & (lane < (g_idx + 1) * EXPERTS_PER_GROUP)
    el = jnp.where(in_group, logits, -jnp.inf)
    v1 = jnp.max(el, axis=-1, keepdims=True)
    i1 = first_argmax(el, v1)
    el2 = jnp.where(lane == i1, -jnp.inf, el)
    v2 = jnp.max(el2, axis=-1, keepdims=True)
    i2 = first_argmax(el2, v2)
    e2 = jnp.exp(v2 - v1)
    den = 1.0 + e2
    w1 = (1.0 / den) * g_val
    w2 = (e2 / den) * g_val

    pick1 = lane == i1
    pick2 = lane == i2
    onehot = jnp.where(pick1 | pick2, 1.0, 0.0).astype(BF16)
    seen = jnp.dot(ltri_ref[...], onehot, preferred_element_type=F32) + cnt_ref[...]
    rank1 = jnp.sum(jnp.where(pick1, seen, 0.0), axis=-1, keepdims=True) - 1.0
    rank2 = jnp.sum(jnp.where(pick2, seen, 0.0), axis=-1, keepdims=True) - 1.0
    cnt_ref[...] = seen[seen.shape[0] - 1:, :]
    cols = (w1, w2, i1.astype(F32), i2.astype(F32), rank1, rank2)
    route = jnp.zeros(logits.shape, F32)
    for c, val in enumerate(cols):
        route = jnp.where(lane == c, val, route)
    route_ref[...] = route


def _mix(o_diff, o_sb, proj, x2, w_brd, w_brs, w_out, g_moe, wr_hi, wr_lo, b_r, *, tm, gate_col0):
    t_tok, d = x2.shape
    wd_in = o_diff.shape[1]
    ws_in = o_sb.shape[1]
    gblk = gate_col0 // d
    r = np.arange(tm)
    ltri = jnp.asarray(r[None, :] <= r[:, None], dtype=BF16)
    const = lambda shape: pl.BlockSpec(shape, lambda i: (0, 0), pipeline_mode=pl.Buffered(1))
    return pl.pallas_call(
        _mix_kernel,
        grid=(t_tok // tm,),
        in_specs=[
            pl.BlockSpec((tm, wd_in), lambda i: (i, 0)),
            pl.BlockSpec((tm, ws_in), lambda i: (i, 0)),
            pl.BlockSpec((tm, d), lambda i: (i, gblk)),
            pl.BlockSpec((tm, d), lambda i: (i, gblk + 1)),
            pl.BlockSpec((tm, d), lambda i: (i, 0)),
            const(w_brd.shape), const(w_brs.shape), const(w_out.shape),
            const(g_moe.shape), const(wr_hi.shape), const(wr_lo.shape), const(b_r.shape),
            const(ltri.shape),
        ],
        out_specs=[
            pl.BlockSpec((tm, d), lambda i: (i, 0)),
            pl.BlockSpec((tm * SLAB_PITCH, LANES), lambda i: (i, 0)),
            pl.BlockSpec((tm, LANES), lambda i: (i, 0)),
            pl.BlockSpec((1, LANES), lambda i: (0, 0)),
        ],
        out_shape=[
            jax.ShapeDtypeStruct((t_tok, d), F32),
            jax.ShapeDtypeStruct((t_tok * SLAB_PITCH, LANES), F32),
            jax.ShapeDtypeStruct((t_tok, LANES), F32),
            jax.ShapeDtypeStruct((1, LANES), F32),
        ],
        compiler_params=_cparams(("arbitrary",)),
        name="mix",
    )(o_diff, o_sb, proj, proj, x2, w_brd, w_brs, w_out, g_moe, wr_hi, wr_lo, b_r, ltri)


def _gather_rows(src_hbm, idx_ref, base, dst, sem, n_rows):
    for r in range(n_rows):
        row = idx_ref[base + r]
        pltpu.make_async_copy(src_hbm.at[pl.ds(row, 1), :], dst.at[pl.ds(r, 1), :], sem).start(priority=r % 2)


def _wait_rows(src_hbm, dst, sem):
    pltpu.make_async_copy(src_hbm.at[pl.ds(0, dst.shape[0]), :], dst, sem).wait()


def _gather_slabs(src_hbm, idx_ref, base, dst, sem, n_items):
    for r in range(n_items):
        tok = idx_ref[base + r]
        pltpu.make_async_copy(src_hbm.at[pl.ds(tok * SLAB_PITCH, SLAB_ROWS), :],
                              dst.at[pl.ds(r * SLAB_PITCH, SLAB_ROWS), :], sem).start(priority=r % 2)


def _wait_slabs(src_hbm, dst, sem, n_items):
    n = n_items * SLAB_ROWS
    pltpu.make_async_copy(src_hbm.at[pl.ds(0, n), :], dst.at[pl.ds(0, n), :], sem).wait()


def _experts_kernel(te_ref, nused_ref, src_ref, hn_hbm, wg_ref, wu_ref, wd_ref, y_ref, xbuf, sem):
    i = pl.program_id(0)
    tm = y_ref.shape[0]
    n_used = nused_ref[0]
    slot = i % 2

    @pl.when(i == 0)
    def _():
        _gather_slabs(hn_hbm, src_ref, 0, xbuf.at[0], sem.at[0], tm)

    @pl.when(i + 1 < n_used)
    def _():
        _gather_slabs(hn_hbm, src_ref, (i + 1) * tm, xbuf.at[1 - slot], sem.at[1 - slot], tm)

    @pl.when(i < n_used)
    def _():
        _wait_slabs(hn_hbm, xbuf.at[slot], sem.at[slot], tm)
        x = jnp.concatenate([xbuf[slot, pl.ds(c, tm, stride=SLAB_PITCH), :] for c in range(SLAB_ROWS)],
                            axis=1).astype(BF16)
        g = jnp.dot(x, wg_ref[0], preferred_element_type=F32)
        u = jnp.dot(x, wu_ref[0], preferred_element_type=F32)
        a = (jax.nn.silu(g) * u).astype(BF16)
        y_ref[...] = jnp.dot(a, wd_ref[0], preferred_element_type=F32)

    @pl.when(i >= n_used)
    def _():
        y_ref[...] = jnp.zeros(y_ref.shape, y_ref.dtype)


def _experts(tile_expert, n_used, src, hn_slabs, w_g, w_u, w_d, *, tm):
    n_rows = src.shape[0]
    n_e, d, ff = w_g.shape
    assert d == SLAB_ROWS * LANES
    grid_spec = pltpu.PrefetchScalarGridSpec(
        num_scalar_prefetch=3,
        grid=(n_rows // tm,),
        in_specs=[
            pl.BlockSpec(memory_space=pl.ANY),
            pl.BlockSpec((1, d, ff), lambda i, te, nu, sr: (te[i], 0, 0)),
            pl.BlockSpec((1, d, ff), lambda i, te, nu, sr: (te[i], 0, 0)),
            pl.BlockSpec((1, ff, d), lambda i, te, nu, sr: (te[i], 0, 0)),
        ],
        out_specs=pl.BlockSpec((tm, d), lambda i, te, nu, sr: (i, 0)),
        scratch_shapes=[pltpu.VMEM((2, tm * SLAB_PITCH, LANES), F32), pltpu.SemaphoreType.DMA((2,))],
    )
    return pl.pallas_call(
        _experts_kernel,
        grid_spec=grid_spec,
        out_shape=jax.ShapeDtypeStruct((n_rows, d), F32),
        compiler_params=pltpu.CompilerParams(dimension_semantics=("arbitrary",),
                                             vmem_limit_bytes=VMEM_LIMIT, disable_bounds_checks=True),
        name="experts",
    )(tile_expert, n_used, src, hn_slabs, w_g, w_u, w_d)


def _combine_kernel(d1_ref, d2_ref, y_hbm, x1_ref, route_ref, gf_ref, o_ref, ybuf, sem):
    i = pl.program_id(0)
    n = pl.num_programs(0)
    tm = x1_ref.shape[0]
    slot = i % 2

    def gather(tile, s):
        _gather_rows(y_hbm, d1_ref, tile * tm, ybuf.at[s, 0], sem.at[s], tm)
        _gather_rows(y_hbm, d2_ref, tile * tm, ybuf.at[s, 1], sem.at[s], tm)

    @pl.when(i == 0)
    def _():
        gather(0, 0)

    @pl.when(i + 1 < n)
    def _():
        gather(i + 1, 1 - slot)

    _wait_rows(y_hbm, ybuf.at[slot, 0], sem.at[slot])
    _wait_rows(y_hbm, ybuf.at[slot, 1], sem.at[slot])
    route = route_ref[...]
    x2 = x1_ref[...] + route[:, 0:1] * ybuf[slot, 0] + route[:, 1:2] * ybuf[slot, 1]
    o_ref[...] = _rms(x2, gf_ref[...])


def _combine(d1, d2, y, x1, route, g_final, *, tm):
    t_tok, d = x1.shape
    grid_spec = pltpu.PrefetchScalarGridSpec(
        num_scalar_prefetch=2,
        grid=(t_tok // tm,),
        in_specs=[
            pl.BlockSpec(memory_space=pl.ANY),
            pl.BlockSpec((tm, d), lambda i, a, b: (i, 0)),
            pl.BlockSpec((tm, LANES), lambda i, a, b: (i, 0)),
            pl.BlockSpec((1, d), lambda i, a, b: (0, 0)),
        ],
        out_specs=pl.BlockSpec((tm, d), lambda i, a, b: (i, 0)),
        scratch_shapes=[pltpu.VMEM((2, 2, tm, d), F32), pltpu.SemaphoreType.DMA((2,))],
    )
    return pl.pallas_call(
        _combine_kernel,
        grid_spec=grid_spec,
        out_shape=jax.ShapeDtypeStruct((t_tok, d), F32),
        compiler_params=pltpu.CompilerParams(dimension_semantics=("arbitrary",),
                                             vmem_limit_bytes=VMEM_LIMIT, disable_bounds_checks=True),
        name="combine",
    )(d1, d2, y, x1, route, g_final)


def _pair_rows_kernel(route_ref, off_ref, d_ref):
    route = route_ref[...]
    off = off_ref[...]
    lane = lax.broadcasted_iota(jnp.int32, route.shape, 1)
    lane_f = lane.astype(F32)

    def row_of(e_col, rank_col):
        first = jnp.sum(jnp.where(lane_f == route[:, e_col:e_col + 1], off, 0.0), axis=-1, keepdims=True)
        return first + route[:, rank_col:rank_col + 1]

    d = jnp.where(lane == 0, row_of(2, 4), jnp.where(lane == 1, row_of(3, 5), 0.0))
    d_ref[...] = d.astype(jnp.int32)


def _pair_rows(route, off_row, *, tm):
    t_tok = route.shape[0]
    return pl.pallas_call(
        _pair_rows_kernel,
        grid=(t_tok // tm,),
        in_specs=[pl.BlockSpec((tm, LANES), lambda i: (i, 0)), pl.BlockSpec((1, LANES), lambda i: (0, 0))],
        out_specs=pl.BlockSpec((tm, LANES), lambda i: (i, 0)),
        out_shape=jax.ShapeDtypeStruct((t_tok, LANES), jnp.int32),
        compiler_params=_cparams(("parallel",)),
        name="pair_rows",
    )(route, off_row)


def _dispatch_plan(route, counts, *, tm, n_tiles):
    t_tok = route.shape[0]
    cnt = counts[0, :N_EXPERTS].astype(jnp.int32)
    tiles_per_e = (cnt + tm - 1) // tm
    tile_end = jnp.cumsum(tiles_per_e)
    off = (tile_end - tiles_per_e) * tm
    off_row = jnp.pad(off.astype(F32), (0, LANES - N_EXPERTS)).reshape(1, LANES)
    pair_rows = _pair_rows(route, off_row, tm=_tile(t_tok, 2048))
    d1 = pair_rows[:, 0]
    d2 = pair_rows[:, 1]
    tok = jnp.arange(t_tok, dtype=jnp.int32)
    src = jnp.zeros((n_tiles * tm,), jnp.int32).at[jnp.concatenate([d1, d2])].add(
        jnp.concatenate([tok, tok]))
    tile_ids = jnp.arange(n_tiles, dtype=jnp.int32)
    tile_expert = jnp.minimum(
        jnp.sum((tile_end[None, :] <= tile_ids[:, None]).astype(jnp.int32), axis=1), N_EXPERTS - 1)
    return d1, d2, src, tile_expert, tile_end[-1:].astype(jnp.int32)


def _tile(n, pref):
    t = min(n, pref)
    assert n % t == 0, (n, t)
    return t


def kernel(x, positions, g_mix, w_in, w_gate, lambda_params, g_subln, w_br_diff, w_br_sb, w_out,
           g_moe, w_group, b_group, w_router, b_router, w_up_gate, w_up, w_down, g_final):
    b, s_len, d = x.shape
    t_tok = b * s_len
    depth = w_in.shape[0]
    assert depth == 1
    l = 0
    x2 = x.reshape(t_tok, d)

    inv_freq = ROPE_THETA ** (-jnp.arange(0, ROT_DIM, 2, dtype=F32) / ROT_DIM)
    lane = np.arange(LANES)
    invf = inv_freq[(lane % DIFF_HEAD_DIM) % (ROT_DIM // 2)].reshape(1, LANES)
    posf = positions.astype(F32).reshape(t_tok, 1)

    in_w = w_in.shape[2]
    w_cat = jnp.concatenate([w_in[l].astype(BF16), w_gate[l].astype(BF16)], axis=1)
    tn = 2048
    qk_w = N_HEADS_DIFF * 2 * DIFF_HEAD_DIM
    sbq_col0 = 2 * qk_w + N_HEADS_DIFF * DIFF_V_DIM
    assert (2 * qk_w) % tn == 0 and (in_w - 2 * qk_w) % tn == 0
    proj = _proj(x2, posf, invf, g_mix[l].reshape(1, d), w_cat,
                 tm=_tile(t_tok, 1024), tn=tn,
                 n_rope_tiles=2 * qk_w // tn, n_plain_tiles=(in_w - 2 * qk_w) // tn,
                 dq_cols=(0, qk_w), sbq_cols=(sbq_col0, sbq_col0 + N_HEADS_SB * SB_HEAD_DIM))
    proj3 = proj.reshape(b, s_len, proj.shape[1])

    o_diff = _dattn(proj3, lambda_params[l], g_subln[l].reshape(1, DIFF_V_DIM), tq=_tile(s_len, 512),
                    heads_per_step=2)
    tq = _tile(s_len, 256)
    r = np.arange(tq)
    tri = jnp.asarray(r[:, None] > r[None, :], dtype=BF16)
    sb_col0 = (2 * qk_w + N_HEADS_DIFF * DIFF_V_DIM) // LANES
    o_sb = _sbattn(proj3, tri, tq=tq, col0=sb_col0, heads_per_step=8)

    w_r = jnp.concatenate([w_router[l], w_group[l]], axis=1)
    w_r = jnp.pad(w_r, ((0, 0), (0, LANES - w_r.shape[1])))
    wr_hi = w_r.astype(BF16)
    wr_lo = (w_r - wr_hi.astype(F32)).astype(BF16)
    b_r = jnp.pad(jnp.concatenate([b_router[l], b_group[l]]), (0, LANES - N_EXPERTS - N_GROUPS))
    x1, hn, route, counts = _mix(
        o_diff.reshape(t_tok, -1), o_sb.reshape(t_tok, -1), proj, x2,
        w_br_diff[l].astype(BF16), w_br_sb[l].astype(BF16), w_out[l].astype(BF16),
        g_moe[l].reshape(1, d), wr_hi, wr_lo, b_r.reshape(1, LANES),
        tm=_tile(t_tok, 256), gate_col0=in_w)

    tm_e = _tile(t_tok, 512)
    n_tiles = 2 * t_tok // tm_e + N_EXPERTS
    d1, d2, src, tile_expert, n_used = _dispatch_plan(route, counts, tm=tm_e, n_tiles=n_tiles)
    y = _experts(tile_expert, n_used, src, hn,
                 w_up_gate[l].astype(BF16), w_up[l].astype(BF16), w_down[l].astype(BF16), tm=tm_e)
    out = _combine(d1, d2, y, x1, route, g_final.reshape(1, d), tm=_tile(t_tok, 256))
    return out.reshape(b, s_len, d)
```

```python
import functools

import numpy as np
import jax
import jax.numpy as jnp
from jax import lax
from jax.experimental import pallas as pl
from jax.experimental.pallas import tpu as pltpu

F32 = jnp.float32
BF16 = jnp.bfloat16

N_HEADS_DIFF = 8
DIFF_HEAD_DIM = 64
DIFF_V_DIM = 128
N_HEADS_SB = 8
SB_HEAD_DIM = 128
ROT_DIM = 16
ROPE_THETA = 500000.0
N_GROUPS = 4
EXPERTS_PER_GROUP = 8
N_EXPERTS = 32
EPS = 1e-6
NEG_INF = float(np.finfo(np.float32).min)
SB_CUTOFF = -150.5
LAMBDA_INIT = 0.8 - 0.6 * float(np.exp(-0.3 * 0))
LOG2E = float(np.log2(np.e))

LANES = 128
MXU_N = 256
SLAB_ROWS = 16
SLAB_PITCH = 20
VMEM_LIMIT = 56 * 1024 * 1024
MIX_VMEM_LIMIT = 62 * 1024 * 1024


def _cparams(sem):
    return pltpu.CompilerParams(dimension_semantics=sem, vmem_limit_bytes=VMEM_LIMIT)


def _rms(x, g):
    return x * lax.rsqrt(jnp.mean(x * x, axis=-1, keepdims=True) + EPS) * g


def _proj_kernel(x_ref, pos_ref, invf_ref, g_ref, w_ref, o_ref, h_ref, c_ref, s1_ref, s2_ref,
                 *, n_rope_tiles, n_plain_tiles, dq_cols, sbq_cols):
    j = pl.program_id(1)
    tn = o_ref.shape[1]

    @pl.when(j == 0)
    def _():
        h_ref[...] = _rms(x_ref[...], g_ref[...]).astype(BF16)
        ang = pos_ref[...] * invf_ref[...]
        lane = lax.broadcasted_iota(jnp.int32, ang.shape, 1) % DIFF_HEAD_DIM
        cos = jnp.cos(ang)
        sin = jnp.sin(ang)
        half = ROT_DIM // 2
        c_ref[...] = jnp.where(lane < ROT_DIM, cos, 1.0)
        s1_ref[...] = jnp.where(lane < half, -sin, 0.0)
        s2_ref[...] = jnp.where((lane >= half) & (lane < ROT_DIM), sin, 0.0)

    def chunks():
        h = h_ref[...]
        for c in range(tn // MXU_N):
            cols = slice(c * MXU_N, (c + 1) * MXU_N)
            yield cols, j * tn + c * MXU_N, jnp.dot(h, w_ref[:, cols], preferred_element_type=F32)

    def scale_in(col0, col_range, value):
        return jnp.where((col0 >= col_range[0]) & (col0 < col_range[1]), value, 1.0).astype(F32)

    @pl.when(j < n_rope_tiles)
    def _():
        half = ROT_DIM // 2
        for cols, col0, acc in chunks():
            qscale = scale_in(col0, dq_cols, DIFF_HEAD_DIM ** -0.5 * LOG2E)
            parts = []
            for c in range(MXU_N // LANES):
                t = acc[:, c * LANES:(c + 1) * LANES]
                up = pltpu.roll(t, LANES - half, axis=1)
                dn = pltpu.roll(t, half, axis=1)
                parts.append((t * c_ref[...] + up * s1_ref[...] + dn * s2_ref[...]) * qscale)
            o_ref[:, cols] = jnp.concatenate(parts, axis=1).astype(o_ref.dtype)

    @pl.when((j >= n_rope_tiles) & (j < n_rope_tiles + n_plain_tiles))
    def _():
        for cols, col0, acc in chunks():
            scale = scale_in(col0, sbq_cols, SB_HEAD_DIM ** -0.5 * LOG2E)
            o_ref[:, cols] = (acc * scale).astype(o_ref.dtype)

    @pl.when(j >= n_rope_tiles + n_plain_tiles)
    def _():
        for cols, _, acc in chunks():
            o_ref[:, cols] = jax.nn.sigmoid(acc).astype(o_ref.dtype)


def _proj(x2, posf, invf, g_mix, w_cat, *, tm, tn, n_rope_tiles, n_plain_tiles, dq_cols, sbq_cols):
    t_tok, d = x2.shape
    n = w_cat.shape[1]
    kern = functools.partial(_proj_kernel, n_rope_tiles=n_rope_tiles, n_plain_tiles=n_plain_tiles,
                             dq_cols=dq_cols, sbq_cols=sbq_cols)
    return pl.pallas_call(
        kern,
        grid=(t_tok // tm, n // tn),
        in_specs=[
            pl.BlockSpec((tm, d), lambda i, j: (i, 0)),
            pl.BlockSpec((tm, 1), lambda i, j: (i, 0)),
            pl.BlockSpec((1, LANES), lambda i, j: (0, 0)),
            pl.BlockSpec((1, d), lambda i, j: (0, 0)),
            pl.BlockSpec((d, tn), lambda i, j: (0, j)),
        ],
        out_specs=pl.BlockSpec((tm, tn), lambda i, j: (i, j)),
        out_shape=jax.ShapeDtypeStruct((t_tok, n), BF16),
        scratch_shapes=[
            pltpu.VMEM((tm, d), BF16),
            pltpu.VMEM((tm, LANES), F32),
            pltpu.VMEM((tm, LANES), F32),
            pltpu.VMEM((tm, LANES), F32),
        ],
        compiler_params=_cparams(("parallel", "arbitrary")),
        name="proj",
    )(x2, posf, invf, g_mix, w_cat)


def _dattn_kernel(q_ref, k_ref, v_ref, lp_ref, g_ref, o_ref, s_ref, ml_ref, mb_ref, acc_ref, vx_ref, *, tq):
    s_len = q_ref.shape[1]
    n_heads = q_ref.shape[2] // LANES
    nq = s_len // tq
    nc = tq // LANES
    lp = lp_ref[...]
    lam = (jnp.exp(jnp.sum(lp[0:1] * lp[1:2], axis=-1, keepdims=True))
           - jnp.exp(jnp.sum(lp[2:3] * lp[3:4], axis=-1, keepdims=True)) + LAMBDA_INIT)
    lane = lax.broadcasted_iota(jnp.int32, (tq, LANES), 1)
    row = lax.broadcasted_iota(jnp.int32, (2 * tq, tq), 0) % tq
    col = lax.broadcasted_iota(jnp.int32, (2 * tq, tq), 1)
    causal = col <= row
    heads = [(hd, slice(hd * LANES, (hd + 1) * LANES)) for hd in range(n_heads)]
    for hd, cols in heads:
        vx_ref[hd, :, :DIFF_V_DIM] = v_ref[0, :, cols]
        vx_ref[hd, :, DIFF_V_DIM:] = jnp.ones((s_len, LANES), vx_ref.dtype)

    def rows(i):
        return pl.ds(pl.multiple_of(i * tq, tq), tq)

    def q_body(qi, carry):
        qqs = []
        for hd, cols in heads:
            q = q_ref[0, rows(qi), cols]
            zero = jnp.zeros_like(q)
            qqs.append(jnp.concatenate([jnp.where(lane < DIFF_HEAD_DIM, q, zero),
                                        jnp.where(lane >= DIFF_HEAD_DIM, q, zero)], axis=0))
        ml_ref[...] = jnp.full(ml_ref.shape, NEG_INF, F32)

        def scores(kj, masked):
            for hd, cols in heads:
                k = k_ref[0, rows(kj), cols]
                s = lax.dot_general(qqs[hd], k, (((1,), (1,)), ((), ())), preferred_element_type=F32)
                if masked:
                    s = jnp.where(causal, s, NEG_INF)
                s_ref[hd, kj] = s
                m = ml_ref[hd]
                for c in range(nc):
                    m = jnp.maximum(m, s[:, c * LANES:(c + 1) * LANES])
                ml_ref[hd] = m

        def s_body(kj, c):
            scores(kj, False)
            return c

        lax.fori_loop(0, qi, s_body, 0)
        scores(qi, True)

        for hd, _ in heads:
            mb_ref[hd] = jnp.broadcast_to(jnp.max(ml_ref[hd], axis=-1, keepdims=True), mb_ref.shape[1:])
        acc_ref[...] = jnp.zeros(acc_ref.shape, F32)

        def av_body(kj, c):
            for hd, _ in heads:
                mb = mb_ref[hd]
                p = jnp.concatenate(
                    [jnp.exp2(s_ref[hd, kj, :, c * LANES:(c + 1) * LANES] - mb) for c in range(nc)], axis=1)
                acc_ref[hd] += jnp.dot(p.astype(BF16), vx_ref[hd, rows(kj), :], preferred_element_type=F32)
            return c

        lax.fori_loop(0, qi + 1, av_body, 0)

        for hd, cols in heads:
            o = acc_ref[hd, :, :DIFF_V_DIM] / acc_ref[hd, :, DIFF_V_DIM:]
            od = o[:tq] - lam * o[tq:]
            od = _rms(od, g_ref[...]) * (1.0 - LAMBDA_INIT)
            o_ref[0, rows(qi), cols] = od.astype(o_ref.dtype)
        return carry

    lax.fori_loop(0, nq, q_body, 0)


def _dattn(proj3, lam_params, g_subln, *, tq, heads_per_step):
    b, s_len, _ = proj3.shape
    h = N_HEADS_DIFF
    hp = heads_per_step
    kern = functools.partial(_dattn_kernel, tq=tq)
    blk = lambda off: pl.BlockSpec((1, s_len, hp * LANES), lambda bi, hi: (bi, 0, off // hp + hi))
    return pl.pallas_call(
        kern,
        grid=(b, h // hp),
        in_specs=[
            blk(0), blk(h), blk(2 * h),
            pl.BlockSpec(lam_params.shape, lambda bi, hi: (0, 0)),
            pl.BlockSpec((1, DIFF_V_DIM), lambda bi, hi: (0, 0)),
        ],
        out_specs=pl.BlockSpec((1, s_len, hp * LANES), lambda bi, hi: (bi, 0, hi)),
        out_shape=jax.ShapeDtypeStruct((b, s_len, h * DIFF_V_DIM), BF16),
        scratch_shapes=[
            pltpu.VMEM((hp, s_len // tq, 2 * tq, tq), F32),
            pltpu.VMEM((hp, 2 * tq, LANES), F32),
            pltpu.VMEM((hp, 2 * tq, LANES), F32),
            pltpu.VMEM((hp, 2 * tq, DIFF_V_DIM + LANES), F32),
            pltpu.VMEM((hp, s_len, DIFF_V_DIM + LANES), BF16),
        ],
        compiler_params=_cparams(("parallel", "parallel")),
        name="dattn",
    )(proj3, proj3, proj3, lam_params, g_subln)


def _sb_terms(q, k, u, strict):
    z = lax.dot_general(q, k, (((1,), (1,)), ((), ())), preferred_element_type=F32)
    l1p = jnp.log2(1.0 + jnp.exp2(-jnp.abs(z)))
    lom = jnp.minimum(-z, 0.0) - l1p
    log_beta = lom + z
    if strict is not None:
        lom = jnp.where(strict, lom, 0.0)
    hi = lom.astype(BF16)
    lo = (lom - hi.astype(F32)).astype(BF16)
    suffix = jnp.dot(hi, u, preferred_element_type=F32) + jnp.dot(lo, u, preferred_element_type=F32)
    return log_beta, lom, suffix


def _lane_rowsum(x):
    return jnp.broadcast_to(jnp.sum(x, axis=-1, keepdims=True), (x.shape[0], LANES))


def _exp_shifted(x, rb):
    nc = x.shape[1] // LANES
    return jnp.concatenate([jnp.exp2(x[:, c * LANES:(c + 1) * LANES] + rb) for c in range(nc)], axis=1)


def _sbattn_kernel(q_ref, k_ref, v_ref, u_ref, o_ref, rb_ref, acc_ref, *, tq):
    s_len = q_ref.shape[1]
    n_heads = q_ref.shape[2] // LANES
    nq = s_len // tq
    row = lax.broadcasted_iota(jnp.int32, (tq, tq), 0)
    col = lax.broadcasted_iota(jnp.int32, (tq, tq), 1)
    strict = col < row
    heads = [(hd, slice(hd * LANES, (hd + 1) * LANES)) for hd in range(n_heads)]

    def rows(i):
        return pl.ds(pl.multiple_of(i * tq, tq), tq)

    def q_tile(qi, first):
        u = u_ref[...]
        rmax = None
        for hd, cols in heads:
            q = q_ref[0, rows(qi), cols]
            log_beta, lom, suffix = _sb_terms(q, k_ref[0, rows(qi), cols], u, strict)
            a = jnp.where(strict, jnp.exp2(log_beta + suffix), 0.0)
            acc = jnp.dot(a.astype(BF16), v_ref[0, rows(qi), cols], preferred_element_type=F32)
            rb = _lane_rowsum(lom)
            if first:
                o_ref[0, rows(qi), cols] = acc.astype(o_ref.dtype)
                continue
            log_beta, lom, suffix = _sb_terms(q, k_ref[0, rows(qi - 1), cols], u, None)
            a = _exp_shifted(log_beta + suffix, rb)
            acc_ref[hd] = acc + jnp.dot(a.astype(BF16), v_ref[0, rows(qi - 1), cols],
                                        preferred_element_type=F32)
            rb = rb + _lane_rowsum(lom)
            rb_ref[hd] = rb
            rmax = jnp.max(rb) if rmax is None else jnp.maximum(rmax, jnp.max(rb))
        if first:
            return

        def cond(c):
            n, rmax = c
            return (n <= qi) & (rmax > SB_CUTOFF)

        def body(c):
            n, _ = c
            kj = qi - n
            rmax = None
            for hd, cols in heads:
                log_beta, lom, suffix = _sb_terms(q_ref[0, rows(qi), cols], k_ref[0, rows(kj), cols],
                                                  u_ref[...], None)
                a = _exp_shifted(log_beta + suffix, rb_ref[hd])
                acc_ref[hd] += jnp.dot(a.astype(BF16), v_ref[0, rows(kj), cols], preferred_element_type=F32)
                rb_new = rb_ref[hd] + _lane_rowsum(lom)
                rb_ref[hd] = rb_new
                rmax = jnp.max(rb_new) if rmax is None else jnp.maximum(rmax, jnp.max(rb_new))
            return n + 1, rmax

        lax.while_loop(cond, body, (jnp.int32(2), rmax))
        for hd, cols in heads:
            o_ref[0, rows(qi), cols] = acc_ref[hd].astype(o_ref.dtype)

    q_tile(0, True)

    def q_body(qi, carry):
        q_tile(qi, False)
        return carry

    lax.fori_loop(1, nq, q_body, 0)


def _sbattn(proj3, tri, *, tq, col0, heads_per_step):
    b, s_len, _ = proj3.shape
    h = N_HEADS_SB
    hp = heads_per_step
    kern = functools.partial(_sbattn_kernel, tq=tq)
    blk = lambda off: pl.BlockSpec((1, s_len, hp * LANES), lambda bi, hi: (bi, 0, off // hp + hi))
    return pl.pallas_call(
        kern,
        grid=(b, h // hp),
        in_specs=[
            blk(col0), blk(col0 + h), blk(col0 + 2 * h),
            pl.BlockSpec((tq, tq), lambda bi, hi: (0, 0)),
        ],
        out_specs=pl.BlockSpec((1, s_len, hp * LANES), lambda bi, hi: (bi, 0, hi)),
        out_shape=jax.ShapeDtypeStruct((b, s_len, h * SB_HEAD_DIM), BF16),
        scratch_shapes=[
            pltpu.VMEM((hp, tq, LANES), F32),
            pltpu.VMEM((hp, tq, SB_HEAD_DIM), F32),
        ],
        compiler_params=_cparams(("parallel", "parallel")),
        name="sbattn",
    )(proj3, proj3, proj3, tri)


def _split_bf16(a):
    hi = a.astype(BF16)
    return hi, (a - hi.astype(F32)).astype(BF16)


def _mix_kernel(od_ref, os_ref, g0_ref, g1_ref, x_ref, wd_ref, ws_ref, wo_ref, gm_ref,
                wr_hi_ref, wr_lo_ref, br_ref, ltri_ref, x1_ref, hn_ref, route_ref, cnt_ref):
    @pl.when(pl.program_id(0) == 0)
    def _():
        cnt_ref[...] = jnp.zeros(cnt_ref.shape, F32)

    yd = jnp.dot(od_ref[...], wd_ref[...], preferred_element_type=F32)
    ys = jnp.dot(os_ref[...], ws_ref[...], preferred_element_type=F32)
    mixed = g0_ref[...].astype(F32) * yd + g1_ref[...].astype(F32) * ys
    x1 = x_ref[...] + jnp.dot(mixed.astype(BF16), wo_ref[...], preferred_element_type=F32)
    x1_ref[...] = x1
    hn = _rms(x1, gm_ref[...])
    hn_hi, hn_lo = _split_bf16(hn)
    tm = hn.shape[0]
    hn_f = hn_hi.astype(F32)
    for c in range(SLAB_PITCH):
        part = hn_f[:, c * LANES:(c + 1) * LANES] if c < SLAB_ROWS else jnp.zeros((tm, LANES), F32)
        hn_ref[pl.ds(c, tm, stride=SLAB_PITCH), :] = part

    quad = jnp.dot(jnp.concatenate([hn_hi, hn_lo], axis=0),
                   jnp.concatenate([wr_hi_ref[...], wr_lo_ref[...]], axis=1), preferred_element_type=F32)
    logits = (quad[:tm, :LANES] + quad[:tm, LANES:]) + (quad[tm:, :LANES] + quad[tm:, LANES:]) + br_ref[...]
    lane = lax.broadcasted_iota(jnp.int32, logits.shape, 1)
    big = jnp.int32(LANES)

    def first_argmax(vals, vmax):
        return jnp.min(jnp.where(vals == vmax, lane, big), axis=-1, keepdims=True)

    is_group = (lane >= N_EXPERTS) & (lane < N_EXPERTS + N_GROUPS)
    gl = jnp.where(is_group, logits, -jnp.inf)
    gmax = jnp.max(gl, axis=-1, keepdims=True)
    gexp = jnp.exp(gl - gmax)
    probs = gexp / jnp.sum(gexp, axis=-1, keepdims=True)
    g_val = jnp.max(probs, axis=-1, keepdims=True)
    g_idx = first_argmax(jnp.where(is_group, probs, -jnp.inf), g_val) - N_EXPERTS

    in_group = (lane >= g_idx * EXPERTS_PER_GROUP) & (lane < (g_idx + 1) * EXPERTS_PER_GROUP)
    el = jnp.where(in_group, logits, -jnp.inf)
    v1 = jnp.max(el, axis=-1, keepdims=True)
    i1 = first_argmax(el, v1)
    el2 = jnp.where(lane == i1, -jnp.inf, el)
    v2 = jnp.max(el2, axis=-1, keepdims=True)
    i2 = first_argmax(el2, v2)
    e2 = jnp.exp(v2 - v1)
    den = 1.0 + e2
    w1 = (1.0 / den) * g_val
    w2 = (e2 / den) * g_val

    pick1 = lane == i1
    pick2 = lane == i2
    onehot = jnp.where(pick1 | pick2, 1.0, 0.0).astype(BF16)
    seen = jnp.dot(ltri_ref[...], onehot, preferred_element_type=F32) + cnt_ref[...]
    rank1 = jnp.sum(jnp.where(pick1, seen, 0.0), axis=-1, keepdims=True) - 1.0
    rank2 = jnp.sum(jnp.where(pick2, seen, 0.0), axis=-1, keepdims=True) - 1.0
    cnt_ref[...] = seen[seen.shape[0] - 1:, :]
    cols = (w1, w2, i1.astype(F32), i2.astype(F32), rank1, rank2)
    route = jnp.zeros(logits.shape, F32)
    for c, val in enumerate(cols):
        route = jnp.where(lane == c, val, route)
    route_ref[...] = route


def _mix(o_diff, o_sb, proj, x2, w_brd, w_brs, w_out, g_moe, wr_hi, wr_lo, b_r, *, tm, gate_col0):
    t_tok, d = x2.shape
    wd_in = o_diff.shape[1]
    ws_in = o_sb.shape[1]
    gblk = gate_col0 // d
    r = np.arange(tm)
    ltri = jnp.asarray(r[None, :] <= r[:, None], dtype=BF16)
    const = lambda shape: pl.BlockSpec(shape, lambda i: (0, 0), pipeline_mode=pl.Buffered(1))
    return pl.pallas_call(
        _mix_kernel,
        grid=(t_tok // tm,),
        in_specs=[
            pl.BlockSpec((tm, wd_in), lambda i: (i, 0)),
            pl.BlockSpec((tm, ws_in), lambda i: (i, 0)),
            pl.BlockSpec((tm, d), lambda i: (i, gblk)),
            pl.BlockSpec((tm, d), lambda i: (i, gblk + 1)),
            pl.BlockSpec((tm, d), lambda i: (i, 0)),
            const(w_brd.shape), const(w_brs.shape), const(w_out.shape),
            const(g_moe.shape), const(wr_hi.shape), const(wr_lo.shape), const(b_r.shape),
            const(ltri.shape),
        ],
        out_specs=[
            pl.BlockSpec((tm, d), lambda i: (i, 0)),
            pl.BlockSpec((tm * SLAB_PITCH, LANES), lambda i: (i, 0)),
            pl.BlockSpec((tm, LANES), lambda i: (i, 0)),
            pl.BlockSpec((1, LANES), lambda i: (0, 0)),
        ],
        out_shape=[
            jax.ShapeDtypeStruct((t_tok, d), F32),
            jax.ShapeDtypeStruct((t_tok * SLAB_PITCH, LANES), F32),
            jax.ShapeDtypeStruct((t_tok, LANES), F32),
            jax.ShapeDtypeStruct((1, LANES), F32),
        ],
        compiler_params=pltpu.CompilerParams(dimension_semantics=("arbitrary",),
                                             vmem_limit_bytes=MIX_VMEM_LIMIT),
        name="mix",
    )(o_diff, o_sb, proj, proj, x2, w_brd, w_brs, w_out, g_moe, wr_hi, wr_lo, b_r, ltri)


def _gather_rows(src_hbm, idx_ref, base, dst, sem, n_rows):
    for r in range(n_rows):
        row = idx_ref[base + r]
        pltpu.make_async_copy(src_hbm.at[pl.ds(row, 1), :], dst.at[pl.ds(r, 1), :], sem).start(priority=r % 2)


def _wait_rows(src_hbm, dst, sem):
    pltpu.make_async_copy(src_hbm.at[pl.ds(0, dst.shape[0]), :], dst, sem).wait()


def _gather_slabs(src_hbm, idx_ref, base, dst, sem, n_items):
    for r in range(n_items):
        tok = idx_ref[base + r]
        pltpu.make_async_copy(src_hbm.at[pl.ds(tok * SLAB_PITCH, SLAB_ROWS), :],
                              dst.at[pl.ds(r * SLAB_PITCH, SLAB_ROWS), :], sem).start(priority=r % 2)


def _wait_slabs(src_hbm, dst, sem, n_items):
    n = n_items * SLAB_ROWS
    pltpu.make_async_copy(src_hbm.at[pl.ds(0, n), :], dst.at[pl.ds(0, n), :], sem).wait()


def _experts_kernel(te_ref, nused_ref, src_ref, hn_hbm, wg_ref, wu_ref, wd_ref, y_ref, xbuf, sem):
    i = pl.program_id(0)
    tm = y_ref.shape[0]
    n_used = nused_ref[0]
    slot = i % 2

    @pl.when(i == 0)
    def _():
        _gather_slabs(hn_hbm, src_ref, 0, xbuf.at[0], sem.at[0], tm)

    @pl.when(i + 1 < n_used)
    def _():
        _gather_slabs(hn_hbm, src_ref, (i + 1) * tm, xbuf.at[1 - slot], sem.at[1 - slot], tm)

    @pl.when(i < n_used)
    def _():
        _wait_slabs(hn_hbm, xbuf.at[slot], sem.at[slot], tm)
        x = jnp.concatenate([xbuf[slot, pl.ds(c, tm, stride=SLAB_PITCH), :] for c in range(SLAB_ROWS)],
                            axis=1).astype(BF16)
        g = jnp.dot(x, wg_ref[0], preferred_element_type=F32)
        u = jnp.dot(x, wu_ref[0], preferred_element_type=F32)
        a = (jax.nn.silu(g) * u).astype(BF16)
        y_ref[...] = jnp.dot(a, wd_ref[0], preferred_element_type=F32)

    @pl.when(i >= n_used)
    def _():
        y_ref[...] = jnp.zeros(y_ref.shape, y_ref.dtype)


def _experts(tile_expert, n_used, src, hn_slabs, w_g, w_u, w_d, *, tm):
    n_rows = src.shape[0]
    n_e, d, ff = w_g.shape
    assert d == SLAB_ROWS * LANES
    grid_spec = pltpu.PrefetchScalarGridSpec(
        num_scalar_prefetch=3,
        grid=(n_rows // tm,),
        in_specs=[
            pl.BlockSpec(memory_space=pl.ANY),
            pl.BlockSpec((1, d, ff), lambda i, te, nu, sr: (te[i], 0, 0)),
            pl.BlockSpec((1, d, ff), lambda i, te, nu, sr: (te[i], 0, 0)),
            pl.BlockSpec((1, ff, d), lambda i, te, nu, sr: (te[i], 0, 0)),
        ],
        out_specs=pl.BlockSpec((tm, d), lambda i, te, nu, sr: (i, 0)),
        scratch_shapes=[pltpu.VMEM((2, tm * SLAB_PITCH, LANES), F32), pltpu.SemaphoreType.DMA((2,))],
    )
    return pl.pallas_call(
        _experts_kernel,
        grid_spec=grid_spec,
        out_shape=jax.ShapeDtypeStruct((n_rows, d), F32),
        compiler_params=pltpu.CompilerParams(dimension_semantics=("arbitrary",),
                                             vmem_limit_bytes=VMEM_LIMIT, disable_bounds_checks=True),
        name="experts",
    )(tile_expert, n_used, src, hn_slabs, w_g, w_u, w_d)


def _combine_kernel(d1_ref, d2_ref, y_hbm, x1_ref, route_ref, gf_ref, o_ref, ybuf, sem):
    i = pl.program_id(0)
    n = pl.num_programs(0)
    tm = x1_ref.shape[0]
    slot = i % 2

    def gather(tile, s):
        _gather_rows(y_hbm, d1_ref, tile * tm, ybuf.at[s, 0], sem.at[s], tm)
        _gather_rows(y_hbm, d2_ref, tile * tm, ybuf.at[s, 1], sem.at[s], tm)

    @pl.when(i == 0)
    def _():
        gather(0, 0)

    @pl.when(i + 1 < n)
    def _():
        gather(i + 1, 1 - slot)

    _wait_rows(y_hbm, ybuf.at[slot, 0], sem.at[slot])
    _wait_rows(y_hbm, ybuf.at[slot, 1], sem.at[slot])
    route = route_ref[...]
    x2 = x1_ref[...] + route[:, 0:1] * ybuf[slot, 0] + route[:, 1:2] * ybuf[slot, 1]
    o_ref[...] = _rms(x2, gf_ref[...])


def _combine(d1, d2, y, x1, route, g_final, *, tm):
    t_tok, d = x1.shape
    grid_spec = pltpu.PrefetchScalarGridSpec(
        num_scalar_prefetch=2,
        grid=(t_tok // tm,),
        in_specs=[
            pl.BlockSpec(memory_space=pl.ANY),
            pl.BlockSpec((tm, d), lambda i, a, b: (i, 0)),
            pl.BlockSpec((tm, LANES), lambda i, a, b: (i, 0)),
            pl.BlockSpec((1, d), lambda i, a, b: (0, 0)),
        ],
        out_specs=pl.BlockSpec((tm, d), lambda i, a, b: (i, 0)),
        scratch_shapes=[pltpu.VMEM((2, 2, tm, d), F32), pltpu.SemaphoreType.DMA((2,))],
    )
    return pl.pallas_call(
        _combine_kernel,
        grid_spec=grid_spec,
        out_shape=jax.ShapeDtypeStruct((t_tok, d), F32),
        compiler_params=pltpu.CompilerParams(dimension_semantics=("arbitrary",),
                                             vmem_limit_bytes=VMEM_LIMIT, disable_bounds_checks=True),
        name="combine",
    )(d1, d2, y, x1, route, g_final)


def _pair_rows_kernel(route_ref, off_ref, d_ref):
    route = route_ref[...]
    off = off_ref[...]
    lane = lax.broadcasted_iota(jnp.int32, route.shape, 1)
    lane_f = lane.astype(F32)

    def row_of(e_col, rank_col):
        first = jnp.sum(jnp.where(lane_f == route[:, e_col:e_col + 1], off, 0.0), axis=-1, keepdims=True)
        return first + route[:, rank_col:rank_col + 1]

    d = jnp.where(lane == 0, row_of(2, 4), jnp.where(lane == 1, row_of(3, 5), 0.0))
    d_ref[...] = d.astype(jnp.int32)


def _pair_rows(route, off_row, *, tm):
    t_tok = route.shape[0]
    return pl.pallas_call(
        _pair_rows_kernel,
        grid=(t_tok // tm,),
        in_specs=[pl.BlockSpec((tm, LANES), lambda i: (i, 0)), pl.BlockSpec((1, LANES), lambda i: (0, 0))],
        out_specs=pl.BlockSpec((tm, LANES), lambda i: (i, 0)),
        out_shape=jax.ShapeDtypeStruct((t_tok, LANES), jnp.int32),
        compiler_params=_cparams(("parallel",)),
        name="pair_rows",
    )(route, off_row)


def _dispatch_plan(route, counts, *, tm, n_tiles):
    t_tok = route.shape[0]
    cnt = counts[0, :N_EXPERTS].astype(jnp.int32)
    tiles_per_e = (cnt + tm - 1) // tm
    tile_end = jnp.cumsum(tiles_per_e)
    off = (tile_end - tiles_per_e) * tm
    off_row = jnp.pad(off.astype(F32), (0, LANES - N_EXPERTS)).reshape(1, LANES)
    pair_rows = _pair_rows(route, off_row, tm=_tile(t_tok, 2048))
    d1 = pair_rows[:, 0]
    d2 = pair_rows[:, 1]
    tok = jnp.arange(t_tok, dtype=jnp.int32)
    src = jnp.zeros((n_tiles * tm,), jnp.int32).at[jnp.concatenate([d1, d2])].add(
        jnp.concatenate([tok, tok]))
    tile_ids = jnp.arange(n_tiles, dtype=jnp.int32)
    tile_expert = jnp.minimum(
        jnp.sum((tile_end[None, :] <= tile_ids[:, None]).astype(jnp.int32), axis=1), N_EXPERTS - 1)
    return d1, d2, src, tile_expert, tile_end[-1:].astype(jnp.int32)


def _tile(n, pref):
    t = min(n, pref)
    assert n % t == 0, (n, t)
    return t


def kernel(x, positions, g_mix, w_in, w_gate, lambda_params, g_subln, w_br_diff, w_br_sb, w_out,
           g_moe, w_group, b_group, w_router, b_router, w_up_gate, w_up, w_down, g_final):
    b, s_len, d = x.shape
    t_tok = b * s_len
    depth = w_in.shape[0]
    assert depth == 1
    l = 0
    x2 = x.reshape(t_tok, d)

    inv_freq = ROPE_THETA ** (-jnp.arange(0, ROT_DIM, 2, dtype=F32) / ROT_DIM)
    lane = np.arange(LANES)
    invf = inv_freq[(lane % DIFF_HEAD_DIM) % (ROT_DIM // 2)].reshape(1, LANES)
    posf = positions.astype(F32).reshape(t_tok, 1)

    in_w = w_in.shape[2]
    w_cat = jnp.concatenate([w_in[l].astype(BF16), w_gate[l].astype(BF16)], axis=1)
    tn = 2048
    qk_w = N_HEADS_DIFF * 2 * DIFF_HEAD_DIM
    sbq_col0 = 2 * qk_w + N_HEADS_DIFF * DIFF_V_DIM
    assert (2 * qk_w) % tn == 0 and (in_w - 2 * qk_w) % tn == 0
    proj = _proj(x2, posf, invf, g_mix[l].reshape(1, d), w_cat,
                 tm=_tile(t_tok, 1024), tn=tn,
                 n_rope_tiles=2 * qk_w // tn, n_plain_tiles=(in_w - 2 * qk_w) // tn,
                 dq_cols=(0, qk_w), sbq_cols=(sbq_col0, sbq_col0 + N_HEADS_SB * SB_HEAD_DIM))
    proj3 = proj.reshape(b, s_len, proj.shape[1])

    o_diff = _dattn(proj3, lambda_params[l], g_subln[l].reshape(1, DIFF_V_DIM), tq=_tile(s_len, 512),
                    heads_per_step=2)
    tq = _tile(s_len, 256)
    r = np.arange(tq)
    tri = jnp.asarray(r[:, None] > r[None, :], dtype=BF16)
    sb_col0 = (2 * qk_w + N_HEADS_DIFF * DIFF_V_DIM) // LANES
    o_sb = _sbattn(proj3, tri, tq=tq, col0=sb_col0, heads_per_step=8)

    w_r = jnp.concatenate([w_router[l], w_group[l]], axis=1)
    w_r = jnp.pad(w_r, ((0, 0), (0, LANES - w_r.shape[1])))
    wr_hi = w_r.astype(BF16)
    wr_lo = (w_r - wr_hi.astype(F32)).astype(BF16)
    b_r = jnp.pad(jnp.concatenate([b_router[l], b_group[l]]), (0, LANES - N_EXPERTS - N_GROUPS))
    x1, hn, route, counts = _mix(
        o_diff.reshape(t_tok, -1), o_sb.reshape(t_tok, -1), proj, x2,
        w_br_diff[l].astype(BF16), w_br_sb[l].astype(BF16), w_out[l].astype(BF16),
        g_moe[l].reshape(1, d), wr_hi, wr_lo, b_r.reshape(1, LANES),
        tm=_tile(t_tok, 512), gate_col0=in_w)

    tm_e = _tile(t_tok, 512)
    n_tiles = 2 * t_tok // tm_e + N_EXPERTS
    d1, d2, src, tile_expert, n_used = _dispatch_plan(route, counts, tm=tm_e, n_tiles=n_tiles)
    y = _experts(tile_expert, n_used, src, hn,
                 w_up_gate[l].astype(BF16), w_up[l].astype(BF16), w_down[l].astype(BF16), tm=tm_e)
    out = _combine(d1, d2, y, x1, route, g_final.reshape(1, d), tm=_tile(t_tok, 256))
    return out.reshape(b, s_len, d)
```
